```python
import jax, jax.numpy as jnp
from jax import lax
import numpy as np

D_MODEL = 2048
BATCH = 1
SEQ = 16384
DEPTH = 1
DEC_BATCH = 2
DEC_SEQ = 8192
PAST_LEN = 128

CHUNK = 64
M_HEADS = 4
M_QK = 128
M_V = 256
M_WIDTH = M_HEADS * M_V
GATE_CAP = 15.0
G_HEADS = 4
G_QK = 128
G_V = 256
G_WIDTH = G_HEADS * G_V
G_RANK = 16
G_TAU = 16.0
MIX_WIDTH = M_WIDTH + G_WIDTH
N_EXPERTS = 32
TOP_K = 4
D_FF = 2048
SWIGLU_LIMIT = 7.0
SWIGLU_ALPHA = 1.702
MOE_BLOCK = 128
EPS = 1e-6
IN_SIZES = (M_HEADS * M_QK, M_HEADS * M_QK, M_WIDTH, M_WIDTH, 4 * M_HEADS,
            G_HEADS * G_QK, G_HEADS * G_QK, G_WIDTH, G_WIDTH, 2 * G_RANK)
IN_WIDTH = (M_HEADS * M_QK * 2 + M_WIDTH * 2 + 4 * M_HEADS
            + G_HEADS * G_QK * 2 + G_WIDTH * 2 + 2 * G_RANK)

kernel_name = "hymba_mlstm_gla_moe_bidir_encoder"


def rms_norm(x, w):
    xf = x.astype(jnp.float32)
    y = xf * lax.rsqrt(jnp.mean(xf * xf, axis=-1, keepdims=True) + EPS)
    return (y * w.astype(jnp.float32)).astype(x.dtype)


def head_rms_norm(h, w):
    b, s, nh, d = h.shape
    y = h * lax.rsqrt(jnp.mean(h * h, axis=-1, keepdims=True) + EPS)
    return y.reshape(b, s, nh * d) * w.astype(jnp.float32)


def to_chunks(t):
    b, s, nh, d = t.shape
    return t.reshape(b, s // CHUNK, CHUNK, nh, d).transpose(0, 3, 1, 2, 4)


def from_chunks(t):
    b, nh, nc, l, d = t.shape
    return t.transpose(0, 2, 3, 1, 4).reshape(b, nc * l, nh, d)


def reverse_seq(t):
    return t[:, ::-1]


def mlstm_chunkwise(q, k, v, i_pre, f_pre):
    bsz, nh, nc, l, dk = q.shape
    dv = v.shape[-1]
    log_f = jax.nn.log_sigmoid(f_pre)
    bcum = jnp.cumsum(log_f, axis=-1)
    g = bcum[..., -1]
    w = g[..., None] - bcum + i_pre
    m_loc = jnp.max(w, axis=-1)
    e = jnp.exp(w - m_loc[..., None])
    ek = e[..., None] * k
    kv_loc = jnp.einsum('bhclk,bhclv->bhckv', ek, v)
    n_loc = jnp.sum(ek, axis=3)

    def step(carry, inp):
        c_st, n_st, m_st = carry
        g_c, m_c, kv_c, nk_c = inp
        m_new = jnp.maximum(g_c + m_st, m_c)
        a = jnp.exp(g_c + m_st - m_new)
        bb = jnp.exp(m_c - m_new)
        c_new = a[..., None, None] * c_st + bb[..., None, None] * kv_c
        n_new = a[..., None] * n_st + bb[..., None] * nk_c
        return (c_new, n_new, m_new), (c_st, n_st, m_st)

    init = (jnp.zeros((bsz, nh, dk, dv), jnp.float32),
            jnp.zeros((bsz, nh, dk), jnp.float32),
            jnp.zeros((bsz, nh), jnp.float32))
    xs = (jnp.moveaxis(g, 2, 0), jnp.moveaxis(m_loc, 2, 0),
          jnp.moveaxis(kv_loc, 2, 0), jnp.moveaxis(n_loc, 2, 0))
    _, (c_in, n_in, m_in) = lax.scan(step, init, xs)
    c_in = jnp.moveaxis(c_in, 0, 2)
    n_in = jnp.moveaxis(n_in, 0, 2)
    m_in = jnp.moveaxis(m_in, 0, 2)

    causal = jnp.tril(jnp.ones((l, l), dtype=bool))
    a_intra = bcum[..., :, None] - bcum[..., None, :] + i_pre[..., None, :]
    a_intra = jnp.where(causal, a_intra, -jnp.inf)
    a_inter = bcum + m_in[..., None]
    m_row = jnp.maximum(jnp.max(a_intra, axis=-1), a_inter)
    p = jnp.exp(a_intra - m_row[..., None]) * jnp.einsum('bhcqd,bhcjd->bhcqj', q, k)
    inter = jnp.exp(a_inter - m_row)
    num = (jnp.einsum('bhcqj,bhcjv->bhcqv', p, v)
           + inter[..., None] * jnp.einsum('bhcqd,bhcdv->bhcqv', q, c_in))
    den = jnp.sum(p, axis=-1) + inter * jnp.einsum('bhcqd,bhcd->bhcq', q, n_in)
    return num / jnp.maximum(jnp.abs(den), jnp.exp(-m_row))[..., None]


def gla_chunkwise(q, k, v, log_a):
    bsz, nh, nc, l, dk = q.shape
    dv = v.shape[-1]
    bcum = jnp.cumsum(log_a, axis=3)
    g = bcum[:, :, :, -1]
    k_end = k * jnp.exp(g[:, :, :, None, :] - bcum)
    kv_loc = jnp.einsum('bhcld,bhclv->bhcdv', k_end, v)

    def step(s_st, inp):
        g_c, kv_c = inp
        return jnp.exp(g_c)[..., None] * s_st + kv_c, s_st

    _, s_in = lax.scan(step, jnp.zeros((bsz, nh, dk, dv), jnp.float32),
                       (jnp.moveaxis(g, 2, 0), jnp.moveaxis(kv_loc, 2, 0)))
    s_in = jnp.moveaxis(s_in, 0, 2)
    q_dec = q * jnp.exp(bcum)
    k_inv = k * jnp.exp(-bcum)
    causal = jnp.tril(jnp.ones((l, l), dtype=bool))
    att = jnp.where(causal, jnp.einsum('bhcqd,bhcjd->bhcqj', q_dec, k_inv), 0.0)
    return (jnp.einsum('bhcqj,bhcjv->bhcqv', att, v)
            + jnp.einsum('bhcqd,bhcdv->bhcqv', q_dec, s_in))


def mlstm_group(q, k, v, o_pre, gate_pre, gate_b, norm_w):
    f32 = jnp.float32
    b, s, _ = q.shape
    q = q.astype(f32).reshape(b, s, M_HEADS, M_QK) * (M_QK ** -0.5)
    k = k.astype(f32).reshape(b, s, M_HEADS, M_QK)
    v = v.astype(f32).reshape(b, s, M_HEADS, M_V)
    gates = gate_pre.astype(f32).reshape(b, s, 4, M_HEADS) + gate_b.astype(f32)
    gates = GATE_CAP * jnp.tanh(gates / GATE_CAP)

    def run(qd, kd, vd, ig, fg):
        h = mlstm_chunkwise(to_chunks(qd), to_chunks(kd), to_chunks(vd),
                            to_chunks(ig[..., None])[..., 0], to_chunks(fg[..., None])[..., 0])
        return from_chunks(h)

    h_f = run(q, k, v, gates[:, :, 0], gates[:, :, 1])
    h_b = reverse_seq(run(reverse_seq(q), reverse_seq(k), reverse_seq(v),
                          reverse_seq(gates[:, :, 2]), reverse_seq(gates[:, :, 3])))
    h = head_rms_norm(h_f + h_b, norm_w)
    return h * jax.nn.sigmoid(o_pre.astype(f32))


def gla_group(q, k, v, r, lr, w_lr, b_lr, norm_w):
    f32 = jnp.float32
    b, s, _ = q.shape
    q = q.astype(f32).reshape(b, s, G_HEADS, G_QK) * (G_QK ** -0.5)
    k = k.astype(f32).reshape(b, s, G_HEADS, G_QK)
    v = v.astype(f32).reshape(b, s, G_HEADS, G_V)
    lr = lr.astype(f32).reshape(b, s, 2, G_RANK)
    z = jnp.einsum('bsdr,drk->bsdk', lr, w_lr.astype(f32)) + b_lr.astype(f32)
    log_a = (jax.nn.log_sigmoid(z) / G_TAU).reshape(b, s, 2, G_HEADS, G_QK)

    def run(qd, kd, vd, la):
        return from_chunks(gla_chunkwise(to_chunks(qd), to_chunks(kd), to_chunks(vd), to_chunks(la)))

    o_f = run(q, k, v, log_a[:, :, 0])
    o_b = reverse_seq(run(reverse_seq(q), reverse_seq(k), reverse_seq(v), reverse_seq(log_a[:, :, 1])))
    o = head_rms_norm(o_f + o_b, norm_w)
    return o * jax.nn.silu(r.astype(f32))


def moe_ffn(x, router_w, router_b, w_gate_up, b_gate_up, w_down, b_down):
    b, s, d = x.shape
    t = b * s
    x2d = x.reshape(t, d)
    logits = x2d.astype(jnp.float32) @ router_w.astype(jnp.float32) + router_b.astype(jnp.float32)
    top_vals, top_idx = lax.top_k(logits, TOP_K)
    gates = jax.nn.softmax(top_vals, axis=-1)
    n_assign = t * TOP_K
    flat_e = top_idx.reshape(n_assign).astype(jnp.int32)
    flat_g = gates.reshape(n_assign)
    order = jnp.argsort(flat_e)
    sorted_e = flat_e[order]
    tok = (order // TOP_K).astype(jnp.int32)
    counts = jnp.bincount(flat_e, length=N_EXPERTS)
    padded = ((counts + MOE_BLOCK - 1) // MOE_BLOCK) * MOE_BLOCK
    pad_end = jnp.cumsum(padded)
    pad_start = pad_end - padded
    grp_start = jnp.cumsum(counts) - counts
    rank = jnp.arange(n_assign, dtype=jnp.int32) - grp_start[sorted_e]
    dest = pad_start[sorted_e] + rank
    n_blocks = n_assign // MOE_BLOCK + N_EXPERTS
    rows = n_blocks * MOE_BLOCK
    buf_tok = jnp.full((rows,), t, jnp.int32).at[dest].set(tok)
    buf_gate = jnp.zeros((rows,), jnp.float32).at[dest].set(flat_g[order])
    block_start = jnp.arange(n_blocks, dtype=jnp.int32) * MOE_BLOCK
    block_e = jnp.minimum(jnp.searchsorted(pad_end, block_start, side='right'), N_EXPERTS - 1)
    x_pad = jnp.concatenate([x2d, jnp.zeros((1, d), x2d.dtype)], axis=0)

    def expert_block(args):
        e, tok_ids, gate = args
        xb = x_pad[tok_ids]
        gu = xb @ w_gate_up[e] + b_gate_up[e]
        g_ = jnp.minimum(gu[:, 0::2].astype(jnp.float32), SWIGLU_LIMIT)
        u_ = jnp.clip(gu[:, 1::2].astype(jnp.float32), -SWIGLU_LIMIT, SWIGLU_LIMIT)
        act = ((u_ + 1.0) * g_ * jax.nn.sigmoid(g_ * SWIGLU_ALPHA)).astype(xb.dtype)
        out = act @ w_down[e] + b_down[e]
        return out.astype(jnp.float32) * gate[:, None]

    ys = lax.map(expert_block, (block_e, buf_tok.reshape(n_blocks, MOE_BLOCK),
                                buf_gate.reshape(n_blocks, MOE_BLOCK)))
    y = jax.ops.segment_sum(ys.reshape(rows, d), buf_tok, num_segments=t + 1)[:t]
    return y.astype(x.dtype).reshape(b, s, d)


def encoder_layer(x, norm1_w, w_in, mlstm_gate_b, mlstm_norm_w, gla_w_lr, gla_b_lr, gla_norm_w,
                  w_out, norm2_w, router_w, router_b, w_gate_up, b_gate_up, w_down, b_down):
    h = rms_norm(x, norm1_w)
    proj = h @ w_in
    split_idx = [int(i) for i in np.cumsum(IN_SIZES)[:-1]]
    mq, mk, mv, mo, mg, gq, gk, gv, gr, glr = jnp.split(proj, split_idx, axis=-1)
    a_out = mlstm_group(mq, mk, mv, mo, mg, mlstm_gate_b, mlstm_norm_w)
    b_out = gla_group(gq, gk, gv, gr, glr, gla_w_lr, gla_b_lr, gla_norm_w)
    mixed = jnp.concatenate([a_out, b_out], axis=-1).astype(x.dtype)
    x = x + mixed @ w_out
    x = x + moe_ffn(rms_norm(x, norm2_w), router_w, router_b, w_gate_up, b_gate_up, w_down, b_down)
    return x


def encoder_trunk(x, norm1_w, w_in, mlstm_gate_b, mlstm_norm_w, gla_w_lr, gla_b_lr, gla_norm_w,
                  w_out, norm2_w, router_w, router_b, w_gate_up, b_gate_up, w_down, b_down, final_norm_w):
    for l in range(DEPTH):
        x = encoder_layer(x, norm1_w[l], w_in[l], mlstm_gate_b[l], mlstm_norm_w[l], gla_w_lr[l],
                          gla_b_lr[l], gla_norm_w[l], w_out[l], norm2_w[l], router_w[l], router_b[l],
                          w_gate_up[l], b_gate_up[l], w_down[l], b_down[l])
    return rms_norm(x, final_norm_w)


def setup_inputs(seed: int = 0) -> dict:
    key = jax.random.key(seed)
    ks = jax.random.split(key, 20)
    f32 = jnp.float32

    def nrm(k, shape, scale):
        return jax.random.normal(k, shape, f32) * scale

    x_prompt = nrm(ks[0], (BATCH, SEQ, D_MODEL), 1.0)
    x_sample = nrm(ks[1], (DEC_BATCH, DEC_SEQ, D_MODEL), 1.0)
    norm1_w = 1.0 + nrm(ks[2], (DEPTH, D_MODEL), 0.02)
    w_in = nrm(ks[3], (DEPTH, D_MODEL, IN_WIDTH), D_MODEL ** -0.5)
    f_offset = jnp.linspace(3.0, 6.0, M_HEADS, dtype=f32)
    row_is_forget = jnp.array([0.0, 1.0, 0.0, 1.0], f32)
    mlstm_gate_b = nrm(ks[4], (DEPTH, 4, M_HEADS), 0.1) + row_is_forget[:, None] * f_offset[None, :]
    mlstm_norm_w = 1.0 + nrm(ks[5], (DEPTH, M_WIDTH), 0.02)
    gla_w_lr = nrm(ks[6], (DEPTH, 2, G_RANK, G_HEADS * G_QK), G_RANK ** -0.5)
    gla_b_lr = nrm(ks[7], (DEPTH, 2, G_HEADS * G_QK), 0.1)
    gla_norm_w = 1.0 + nrm(ks[8], (DEPTH, G_WIDTH), 0.02)
    w_out = nrm(ks[9], (DEPTH, MIX_WIDTH, D_MODEL), MIX_WIDTH ** -0.5)
    norm2_w = 1.0 + nrm(ks[10], (DEPTH, D_MODEL), 0.02)
    router_w = nrm(ks[11], (DEPTH, D_MODEL, N_EXPERTS), D_MODEL ** -0.5)
    router_b = nrm(ks[12], (DEPTH, N_EXPERTS), 0.01)
    w_gate_up = nrm(ks[13], (DEPTH, N_EXPERTS, D_MODEL, 2 * D_FF), D_MODEL ** -0.5)
    b_gate_up = nrm(ks[14], (DEPTH, N_EXPERTS, 2 * D_FF), 0.01)
    w_down = nrm(ks[15], (DEPTH, N_EXPERTS, D_FF, D_MODEL), D_FF ** -0.5)
    b_down = nrm(ks[16], (DEPTH, N_EXPERTS, D_MODEL), 0.01)
    final_norm_w = 1.0 + nrm(ks[17], (D_MODEL,), 0.02)
    return {"x_prompt": x_prompt, "x_sample": x_sample, "norm1_w": norm1_w, "w_in": w_in,
            "mlstm_gate_b": mlstm_gate_b, "mlstm_norm_w": mlstm_norm_w, "gla_w_lr": gla_w_lr,
            "gla_b_lr": gla_b_lr, "gla_norm_w": gla_norm_w, "w_out": w_out, "norm2_w": norm2_w,
            "router_w": router_w, "router_b": router_b, "w_gate_up": w_gate_up, "b_gate_up": b_gate_up,
            "w_down": w_down, "b_down": b_down, "final_norm_w": final_norm_w}


def reference(x_prompt, x_sample, norm1_w, w_in, mlstm_gate_b, mlstm_norm_w, gla_w_lr, gla_b_lr,
              gla_norm_w, w_out, norm2_w, router_w, router_b, w_gate_up, b_gate_up, w_down, b_down,
              final_norm_w):
    y_prompt = encoder_trunk(x_prompt, norm1_w, w_in, mlstm_gate_b, mlstm_norm_w, gla_w_lr, gla_b_lr,
                             gla_norm_w, w_out, norm2_w, router_w, router_b, w_gate_up, b_gate_up,
                             w_down, b_down, final_norm_w)
    y_sample = encoder_trunk(x_sample, norm1_w, w_in, mlstm_gate_b, mlstm_norm_w, gla_w_lr, gla_b_lr,
                             gla_norm_w, w_out, norm2_w, router_w, router_b, w_gate_up, b_gate_up,
                             w_down, b_down, final_norm_w)
    return (y_prompt, y_sample)
```

```python
import functools

import jax
import jax.numpy as jnp
from jax import lax
from jax.experimental import pallas as pl
from jax.experimental.pallas import tpu as pltpu

F32 = jnp.float32
BF16 = jnp.bfloat16

D_MODEL = 2048
CHUNK = 64
M_HEADS = 4
M_QK = 128
M_V = 256
M_WIDTH = M_HEADS * M_V
GATE_CAP = 15.0
G_HEADS = 4
G_QK = 128
G_V = 256
G_WIDTH = G_HEADS * G_V
G_RANK = 16
G_TAU = 16.0
TOP_K = 4
SWIGLU_LIMIT = 7.0
SWIGLU_ALPHA = 1.702
EPS = 1e-6
MAIN_WIDTH = 2 * M_HEADS * M_QK + 2 * M_WIDTH + 2 * G_HEADS * G_QK + 2 * G_WIDTH
SMALL_WIDTH = 128
NEG_INF = float("-inf")

VMEM_LIMIT = 48 * 1024 * 1024


def _cparams(sem):
    return pltpu.CompilerParams(dimension_semantics=sem, vmem_limit_bytes=VMEM_LIMIT)


def _log_sigmoid(x):
    return jnp.minimum(x, 0.0) - jnp.log1p(jnp.exp(-jnp.abs(x)))


def _split3(x):
    hi = x.astype(BF16)
    r = x - hi.astype(F32)
    mid = r.astype(BF16)
    lo = (r - mid.astype(F32)).astype(BF16)
    return hi, mid, lo


def _dot(a, b):
    return jnp.dot(a, b, preferred_element_type=F32)


def _dot_nt(a, b):
    return lax.dot_general(a, b, (((1,), (1,)), ((), ())), preferred_element_type=F32)


def _dot_tn(a, b):
    return lax.dot_general(a, b, (((0,), (0,)), ((), ())), preferred_element_type=F32)


def _cum_left(tri, x):
    hi, mid, lo = _split3(x)
    return _dot(tri, hi) + _dot(tri, mid) + _dot(tri, lo)


def _cum_right(x, tri):
    hi, mid, lo = _split3(x)
    return _dot(hi, tri) + _dot(mid, tri) + _dot(lo, tri)


def _inproj_kernel(x_ref, nw_ref, w_ref, ws_ref, wst_ref, main_ref, small_ref, smallt_ref, hn_ref):
    @pl.when(pl.program_id(1) == 0)
    def _():
        x = x_ref[...]
        y = x * lax.rsqrt(jnp.mean(x * x, axis=-1, keepdims=True) + EPS) * nw_ref[...]
        hb = y.astype(BF16)
        hn_ref[...] = hb
        small_ref[...] = _dot(hb, ws_ref[...])
        smallt_ref[...] = _dot_nt(wst_ref[...], hb)

    main_ref[...] = _dot(hn_ref[...], w_ref[...]).astype(BF16)


def _in_proj(x, norm_w, w_main, w_small, w_small_t, bm, bn):
    t = x.shape[0]
    return pl.pallas_call(
        _inproj_kernel,
        grid=(t // bm, MAIN_WIDTH // bn),
        in_specs=[
            pl.BlockSpec((bm, D_MODEL), lambda i, j: (i, 0)),
            pl.BlockSpec((1, D_MODEL), lambda i, j: (0, 0)),
            pl.BlockSpec((D_MODEL, bn), lambda i, j: (0, j)),
            pl.BlockSpec((D_MODEL, SMALL_WIDTH), lambda i, j: (0, 0)),
            pl.BlockSpec((SMALL_WIDTH, D_MODEL), lambda i, j: (0, 0)),
        ],
        out_specs=[
            pl.BlockSpec((bm, bn), lambda i, j: (i, j)),
            pl.BlockSpec((bm, SMALL_WIDTH), lambda i, j: (i, 0)),
            pl.BlockSpec((SMALL_WIDTH, bm), lambda i, j: (0, i)),
        ],
        out_shape=[
            jax.ShapeDtypeStruct((t, MAIN_WIDTH), BF16),
            jax.ShapeDtypeStruct((t, SMALL_WIDTH), F32),
            jax.ShapeDtypeStruct((SMALL_WIDTH, t), F32),
        ],
        scratch_shapes=[pltpu.VMEM((bm, D_MODEL), BF16)],
        compiler_params=_cparams(("arbitrary", "arbitrary")),
        name="in_proj",
    )(x, norm_w, w_main, w_small, w_small_t)


def _chunk_masks(reverse):
    row = lax.broadcasted_iota(jnp.int32, (CHUNK, CHUNK), 0)
    col = lax.broadcasted_iota(jnp.int32, (CHUNK, CHUNK), 1)
    return (col >= row) if reverse else (col <= row)


def _reset_pred(blk, reset_blocks):
    pred = blk == reset_blocks[0]
    for rb in reset_blocks[1:]:
        pred = jnp.logical_or(pred, blk == rb)
    return pred


def _mlstm_kernel(q_ref, k_ref, v_ref, sm_ref, smt_ref, gb_ref, gbt_ref, tri_ref, trit_ref,
                  h_ref, c_ref, n_ref, m_ref, *, reverse, nblocks, reset_blocks, bt):
    i = pl.program_id(0)
    blk = (nblocks - 1 - i) if reverse else i

    @pl.when(_reset_pred(blk, reset_blocks))
    def _():
        c_ref[...] = jnp.zeros_like(c_ref)
        n_ref[...] = jnp.zeros_like(n_ref)
        m_ref[...] = jnp.zeros_like(m_ref)

    d = 1 if reverse else 0
    scale = M_QK ** -0.5
    gates = GATE_CAP * jnp.tanh((sm_ref[...] + gb_ref[...]) / GATE_CAP)
    bcum = _cum_left(tri_ref[...], _log_sigmoid(gates))
    gates_t = GATE_CAP * jnp.tanh((smt_ref[0:16, :] + gbt_ref[...]) / GATE_CAP)
    bcum_t = _cum_right(_log_sigmoid(gates_t), trit_ref[...])
    drow_all = gates_t[8 * d:8 * d + 4, :] - bcum_t[8 * d + 4:8 * d + 8, :]
    mask = _chunk_masks(reverse)
    nchunks = bt // CHUNK
    order = range(nchunks - 1, -1, -1) if reverse else range(nchunks)
    for c in order:
        r0 = c * CHUNK
        for h in range(M_HEADS):
            il = 8 * d + h
            fl = 8 * d + 4 + h
            b = bcum[r0:r0 + CHUNK, fl:fl + 1]
            ic = gates[r0:r0 + CHUNK, il:il + 1]
            g = b[0:1] if reverse else b[CHUNK - 1:CHUNK]
            w = g - b + ic
            m_loc = jnp.max(w, axis=0, keepdims=True)
            e = jnp.exp(w - m_loc)
            qb = q_ref[r0:r0 + CHUNK, h * M_QK:(h + 1) * M_QK]
            kb = k_ref[r0:r0 + CHUNK, h * M_QK:(h + 1) * M_QK]
            vb = v_ref[r0:r0 + CHUNK, h * M_V:(h + 1) * M_V]
            ek = e * kb.astype(F32)
            kv_loc = _dot_tn(ek.astype(BF16), vb)
            n_loc = jnp.sum(ek, axis=0, keepdims=True)

            m_in = m_ref[h]
            c_in = c_ref[h]
            n_in = n_ref[h]
            s = _dot_nt(qb, kb) * scale
            a_intra = jnp.where(mask, b + drow_all[h:h + 1, r0:r0 + CHUNK], NEG_INF)
            a_inter = b + m_in
            m_row = jnp.maximum(jnp.max(a_intra, axis=1, keepdims=True), a_inter)
            p = jnp.exp(a_intra - m_row) * s
            inter = jnp.exp(a_inter - m_row)
            qn = jnp.sum(qb.astype(F32) * n_in, axis=1, keepdims=True) * scale
            den = jnp.sum(p, axis=1, keepdims=True) + inter * qn
            rinv = 1.0 / jnp.maximum(jnp.abs(den), jnp.exp(-m_row))
            qc = _dot(qb, c_in.astype(BF16)) * scale
            hcur = _dot((p * rinv).astype(BF16), vb) + (inter * rinv) * qc
            h_ref[r0:r0 + CHUNK, h * M_V:(h + 1) * M_V] = hcur.astype(h_ref.dtype)

            m_new = jnp.maximum(g + m_in, m_loc)
            aa = jnp.exp(g + m_in - m_new)
            bb = jnp.exp(m_loc - m_new)
            c_ref[h] = aa * c_in + bb * kv_loc
            n_ref[h] = aa * n_in + bb * n_loc
            m_ref[h] = m_new


def _gla_kernel(q_ref, k_ref, v_ref, sm_ref, wlr_ref, blr_ref, tri_ref, o_ref, s_ref,
                *, reverse, nblocks, reset_blocks, bt):
    i = pl.program_id(0)
    blk = (nblocks - 1 - i) if reverse else i

    @pl.when(_reset_pred(blk, reset_blocks))
    def _():
        s_ref[...] = jnp.zeros_like(s_ref)

    scale = G_QK ** -0.5
    z = jnp.dot(sm_ref[...], wlr_ref[...], preferred_element_type=F32,
                precision=lax.Precision.HIGHEST) + blr_ref[...]
    log_a = _log_sigmoid(z) * (1.0 / G_TAU)
    bcum = _cum_left(tri_ref[...], log_a)
    mask = _chunk_masks(reverse)
    nchunks = bt // CHUNK
    order = range(nchunks - 1, -1, -1) if reverse else range(nchunks)
    for c in order:
        r0 = c * CHUNK
        for h in range(G_HEADS):
            b = bcum[r0:r0 + CHUNK, h * G_QK:(h + 1) * G_QK]
            g = b[0:1] if reverse else b[CHUNK - 1:CHUNK]
            qf = q_ref[r0:r0 + CHUNK, h * G_QK:(h + 1) * G_QK].astype(F32)
            kf = k_ref[r0:r0 + CHUNK, h * G_QK:(h + 1) * G_QK].astype(F32)
            vb = v_ref[r0:r0 + CHUNK, h * G_V:(h + 1) * G_V]
            q_dec = (qf * scale * jnp.exp(b)).astype(BF16)
            k_inv = (kf * jnp.exp(-b)).astype(BF16)
            k_end = (kf * jnp.exp(g - b)).astype(BF16)
            st_in = s_ref[h]
            att = jnp.where(mask, _dot_nt(q_dec, k_inv), 0.0)
            o = _dot(att.astype(BF16), vb) + _dot_nt(q_dec, st_in.astype(BF16))
            o_ref[r0:r0 + CHUNK, h * G_V:(h + 1) * G_V] = o.astype(o_ref.dtype)
            s_ref[h] = st_in * jnp.exp(g) + _dot_tn(vb, k_end)


def _block_tri(bt, upper):
    r = jnp.arange(bt)
    same = (r[:, None] // CHUNK) == (r[None, :] // CHUNK)
    tri = (r[None, :] >= r[:, None]) if upper else (r[None, :] <= r[:, None])
    return (same & tri).astype(BF16)


def _reset_blocks(seq_lens, bt, reverse):
    out, off = [], 0
    for s in seq_lens:
        assert s % bt == 0
        out.append((off + s) // bt - 1 if reverse else off // bt)
        off += s
    return tuple(out)


def _mlstm_dir(proj, small, small_t, gate_b, seq_lens, bt, reverse):
    t = proj.shape[0]
    nb = t // bt
    tri = _block_tri(bt, reverse)
    rev = (lambda i: nb - 1 - i) if reverse else (lambda i: i)
    gb = jnp.zeros((1, SMALL_WIDTH), F32).at[0, :16].set(gate_b.reshape(16))
    gbt = gate_b.reshape(16, 1)
    kern = functools.partial(_mlstm_kernel, reverse=reverse, nblocks=nb,
                             reset_blocks=_reset_blocks(seq_lens, bt, reverse), bt=bt)
    return pl.pallas_call(
        kern,
        grid=(nb,),
        in_specs=[
            pl.BlockSpec((bt, 512), lambda i: (rev(i), 0)),
            pl.BlockSpec((bt, 512), lambda i: (rev(i), 1)),
            pl.BlockSpec((bt, 1024), lambda i: (rev(i), 1)),
            pl.BlockSpec((bt, SMALL_WIDTH), lambda i: (rev(i), 0)),
            pl.BlockSpec((SMALL_WIDTH, bt), lambda i: (0, rev(i))),
            pl.BlockSpec((1, SMALL_WIDTH), lambda i: (0, 0)),
            pl.BlockSpec((16, 1), lambda i: (0, 0)),
            pl.BlockSpec((bt, bt), lambda i: (0, 0)),
            pl.BlockSpec((bt, bt), lambda i: (0, 0)),
        ],
        out_specs=pl.BlockSpec((bt, M_WIDTH), lambda i: (rev(i), 0)),
        out_shape=jax.ShapeDtypeStruct((t, M_WIDTH), BF16),
        scratch_shapes=[pltpu.VMEM((M_HEADS, M_QK, M_V), F32),
                        pltpu.VMEM((M_HEADS, 1, M_QK), F32),
                        pltpu.VMEM((M_HEADS, 1, 1), F32)],
        compiler_params=_cparams(("arbitrary",)),
        name="mlstm_bwd" if reverse else "mlstm_fwd",
    )(proj, proj, proj, small, small_t, gb, gbt, tri, tri.T)


def _gla_dir(proj, small, w_lr, b_lr, seq_lens, bt, reverse):
    t = proj.shape[0]
    nb = t // bt
    d = 1 if reverse else 0
    tri = _block_tri(bt, reverse)
    rev = (lambda i: nb - 1 - i) if reverse else (lambda i: i)
    wlr = jnp.zeros((SMALL_WIDTH, G_HEADS * G_QK), F32).at[16 + 16 * d:32 + 16 * d].set(w_lr[d].astype(F32))
    blr = b_lr[d].astype(F32).reshape(1, G_HEADS * G_QK)
    kern = functools.partial(_gla_kernel, reverse=reverse, nblocks=nb,
                             reset_blocks=_reset_blocks(seq_lens, bt, reverse), bt=bt)
    return pl.pallas_call(
        kern,
        grid=(nb,),
        in_specs=[
            pl.BlockSpec((bt, 512), lambda i: (rev(i), 6)),
            pl.BlockSpec((bt, 512), lambda i: (rev(i), 7)),
            pl.BlockSpec((bt, 1024), lambda i: (rev(i), 4)),
            pl.BlockSpec((bt, SMALL_WIDTH), lambda i: (rev(i), 0)),
            pl.BlockSpec((SMALL_WIDTH, G_HEADS * G_QK), lambda i: (0, 0)),
            pl.BlockSpec((1, G_HEADS * G_QK), lambda i: (0, 0)),
            pl.BlockSpec((bt, bt), lambda i: (0, 0)),
        ],
        out_specs=pl.BlockSpec((bt, G_WIDTH), lambda i: (rev(i), 0)),
        out_shape=jax.ShapeDtypeStruct((t, G_WIDTH), BF16),
        scratch_shapes=[pltpu.VMEM((G_HEADS, G_V, G_QK), F32)],
        compiler_params=_cparams(("arbitrary",)),
        name="gla_bwd" if reverse else "gla_fwd",
    )(proj, proj, proj, small, wlr, blr, tri)


def _head_norm(hsum, w, nheads, hd):
    outs = []
    for h in range(nheads):
        x = hsum[:, h * hd:(h + 1) * hd]
        outs.append(x * lax.rsqrt(jnp.mean(x * x, axis=-1, keepdims=True) + EPS))
    return jnp.concatenate(outs, axis=-1) * w


def _outproj_kernel(x_ref, mf_ref, mb_ref, gf_ref, gb_ref, mo_ref, gr_ref, mnw_ref, gnw_ref, wout_ref,
                    n2w_ref, rw_ref, rb_ref, x1_ref, xn_ref, idx_ref, gate_ref, *, n_experts):
    hm = mf_ref[...].astype(F32) + mb_ref[...].astype(F32)
    a_out = _head_norm(hm, mnw_ref[...], M_HEADS, M_V) * jax.nn.sigmoid(mo_ref[...].astype(F32))
    hg = gf_ref[...].astype(F32) + gb_ref[...].astype(F32)
    r = gr_ref[...].astype(F32)
    b_out = _head_norm(hg, gnw_ref[...], G_HEADS, G_V) * (r * jax.nn.sigmoid(r))
    mixed = jnp.concatenate([a_out, b_out], axis=-1).astype(BF16)
    x1 = x_ref[...] + _dot(mixed, wout_ref[...])
    x1_ref[...] = x1
    xn = x1 * lax.rsqrt(jnp.mean(x1 * x1, axis=-1, keepdims=True) + EPS) * n2w_ref[...]
    xn_ref[...] = xn.astype(BF16)
    logits = jnp.dot(xn, rw_ref[...], preferred_element_type=F32,
                     precision=lax.Precision.HIGHEST) + rb_ref[...]
    lane = lax.broadcasted_iota(jnp.int32, logits.shape, 1).astype(F32)
    vals, idxs = [], []
    cur = logits
    for _ in range(TOP_K):
        m = jnp.max(cur, axis=1, keepdims=True)
        sel = jnp.min(jnp.where(cur == m, lane, float(n_experts)), axis=1, keepdims=True)
        vals.append(m)
        idxs.append(sel)
        cur = jnp.where(lane == sel, NEG_INF, cur)
    exps = [jnp.exp(v - vals[0]) for v in vals]
    tot = exps[0] + exps[1] + exps[2] + exps[3]
    gate_ref[...] = jnp.concatenate(exps, axis=1) / tot
    idx_ref[...] = jnp.concatenate(idxs, axis=1).astype(jnp.int32)


def _out_proj(x, hm_f, hm_b, hg_f, hg_b, proj, mnw, gnw, w_out, n2w, router_w, router_b, bm):
    t = x.shape[0]
    n_e = router_w.shape[1]
    row = lambda c: pl.BlockSpec((bm, 1024), lambda i: (i, c))
    full = lambda a: pl.BlockSpec(a.shape, lambda i: (0,) * a.ndim)
    args = (x, hm_f, hm_b, hg_f, hg_b, proj, proj, mnw, gnw, w_out, n2w, router_w, router_b)
    return pl.pallas_call(
        functools.partial(_outproj_kernel, n_experts=n_e),
        grid=(t // bm,),
        in_specs=[pl.BlockSpec((bm, D_MODEL), lambda i: (i, 0)), row(0), row(0), row(0), row(0),
                  row(2), row(5)] + [full(a) for a in args[7:]],
        out_specs=[pl.BlockSpec((bm, D_MODEL), lambda i: (i, 0)),
                   pl.BlockSpec((bm, D_MODEL), lambda i: (i, 0)),
                   pl.BlockSpec((bm, TOP_K), lambda i: (i, 0)),
                   pl.BlockSpec((bm, TOP_K), lambda i: (i, 0))],
        out_shape=[jax.ShapeDtypeStruct((t, D_MODEL), F32),
                   jax.ShapeDtypeStruct((t, D_MODEL), BF16),
                   jax.ShapeDtypeStruct((t, TOP_K), jnp.int32),
                   jax.ShapeDtypeStruct((t, TOP_K), F32)],
        compiler_params=_cparams(("arbitrary",)),
        name="out_proj_router",
    )(*args)


def _moe_kernel(be_ref, nu_ref, x_ref, wg_ref, wu_ref, wd_ref, bg_ref, bu_ref, bd_ref, out_ref, *, nf):
    i = pl.program_id(0)
    f = pl.program_id(1)
    used = i < nu_ref[0]

    @pl.when(jnp.logical_and(jnp.logical_not(used), f == 0))
    def _():
        out_ref[...] = jnp.zeros_like(out_ref)

    @pl.when(used)
    def _():
        xb = x_ref[...]
        g = jnp.minimum(_dot(xb, wg_ref[0]) + bg_ref[0], SWIGLU_LIMIT)
        u = jnp.clip(_dot(xb, wu_ref[0]) + bu_ref[0], -SWIGLU_LIMIT, SWIGLU_LIMIT)
        act = ((u + 1.0) * g * jax.nn.sigmoid(g * SWIGLU_ALPHA)).astype(BF16)
        contrib = _dot(act, wd_ref[0])

        @pl.when(f == 0)
        def _():
            out_ref[...] = contrib + bd_ref[0]

        @pl.when(f != 0)
        def _():
            out_ref[...] += contrib


def _moe_experts(x_sorted, block_e, n_used, wg, wu, wd, bg, bu, bd, bm, bf):
    rows = x_sorted.shape[0]
    nblocks = rows // bm
    d_ff = wg.shape[2]
    nf = d_ff // bf

    def fe(i, f, be, nu):
        return jnp.where(i < nu[0], f, nf - 1)

    grid_spec = pltpu.PrefetchScalarGridSpec(
        num_scalar_prefetch=2,
        grid=(nblocks, nf),
        in_specs=[
            pl.BlockSpec((bm, D_MODEL), lambda i, f, be, nu: (i, 0)),
            pl.BlockSpec((1, D_MODEL, bf), lambda i, f, be, nu: (be[i], 0, fe(i, f, be, nu))),
            pl.BlockSpec((1, D_MODEL, bf), lambda i, f, be, nu: (be[i], 0, fe(i, f, be, nu))),
            pl.BlockSpec((1, bf, D_MODEL), lambda i, f, be, nu: (be[i], fe(i, f, be, nu), 0)),
            pl.BlockSpec((1, 1, bf), lambda i, f, be, nu: (be[i], 0, fe(i, f, be, nu))),
            pl.BlockSpec((1, 1, bf), lambda i, f, be, nu: (be[i], 0, fe(i, f, be, nu))),
            pl.BlockSpec((1, 1, D_MODEL), lambda i, f, be, nu: (be[i], 0, 0)),
        ],
        out_specs=pl.BlockSpec((bm, D_MODEL), lambda i, f, be, nu: (i, 0)),
    )
    return pl.pallas_call(
        functools.partial(_moe_kernel, nf=nf),
        grid_spec=grid_spec,
        out_shape=jax.ShapeDtypeStruct((rows, D_MODEL), F32),
        compiler_params=_cparams(("arbitrary", "arbitrary")),
        name="moe_experts",
    )(block_e, n_used, x_sorted, wg, wu, wd, bg, bu, bd)


def _combine_kernel(x1_ref, y0_ref, y1_ref, y2_ref, y3_ref, gate_ref, fw_ref, out_ref):
    gate = gate_ref[...]
    y = (y0_ref[...] * gate[:, 0:1] + y1_ref[...] * gate[:, 1:2]
         + y2_ref[...] * gate[:, 2:3] + y3_ref[...] * gate[:, 3:4])
    x2 = x1_ref[...] + y
    out_ref[...] = x2 * lax.rsqrt(jnp.mean(x2 * x2, axis=-1, keepdims=True) + EPS) * fw_ref[...]


def _combine(x1, ys, gates, final_w, bm):
    t = x1.shape[0]
    blk = pl.BlockSpec((bm, D_MODEL), lambda i: (i, 0))
    return pl.pallas_call(
        _combine_kernel,
        grid=(t // bm,),
        in_specs=[blk, blk, blk, blk, blk,
                  pl.BlockSpec((bm, TOP_K), lambda i: (i, 0)),
                  pl.BlockSpec((1, D_MODEL), lambda i: (0, 0))],
        out_specs=blk,
        out_shape=jax.ShapeDtypeStruct((t, D_MODEL), F32),
        compiler_params=_cparams(("arbitrary",)),
        name="combine_final_norm",
    )(x1, ys[0], ys[1], ys[2], ys[3], gates, final_w)


def _route(top_idx, n_experts, bm):
    t = top_idx.shape[0]
    n_assign = t * TOP_K
    flat_e = top_idx.reshape(n_assign)
    onehot = (flat_e[:, None] == jnp.arange(n_experts, dtype=jnp.int32)[None, :]).astype(jnp.int32)
    csum = jnp.cumsum(onehot, axis=0)
    rank = jnp.sum((csum - onehot) * onehot, axis=1)
    counts = csum[-1]
    padded = ((counts + bm - 1) // bm) * bm
    pad_end = jnp.cumsum(padded)
    pad_start = pad_end - padded
    dest = (pad_start[flat_e] + rank).astype(jnp.int32)
    nblocks = n_assign // bm + n_experts
    n_used = (pad_end[-1] // bm).astype(jnp.int32).reshape(1)
    block_start = jnp.arange(nblocks, dtype=jnp.int32) * bm
    block_e = jnp.minimum(jnp.searchsorted(pad_end, block_start, side="right"), n_experts - 1)
    last_e = block_e[jnp.maximum(n_used[0] - 1, 0)]
    block_e = jnp.where(jnp.arange(nblocks) < n_used[0], block_e, last_e).astype(jnp.int32)
    tok = jnp.arange(n_assign, dtype=jnp.int32) // TOP_K
    buf_tok = jnp.zeros((nblocks * bm,), jnp.int32).at[dest].set(tok)
    return dest, buf_tok, block_e, n_used


def _trunk(x, seq_lens, norm1_w, w_in, mlstm_gate_b, mlstm_norm_w, gla_w_lr, gla_b_lr, gla_norm_w, w_out,
           norm2_w, router_w, router_b, w_gate_up, b_gate_up, w_down, b_down, final_norm_w,
           *, bm_in=512, bn_in=1024, bt_mix=256, bm_out=256, bm_moe=512, bf_moe=512, bm_comb=256):
    t = x.shape[0]
    n_e = router_w.shape[1]
    w_main = jnp.concatenate([w_in[:, :3072], w_in[:, 3088:6160]], axis=1).astype(BF16)
    w_small = jnp.zeros((D_MODEL, SMALL_WIDTH), F32)
    w_small = w_small.at[:, 0:16].set(w_in[:, 3072:3088]).at[:, 16:48].set(w_in[:, 6160:6192]).astype(BF16)
    proj, small, small_t = _in_proj(x, norm1_w.reshape(1, D_MODEL), w_main, w_small, w_small.T, bm_in, bn_in)

    hm_f = _mlstm_dir(proj, small, small_t, mlstm_gate_b, seq_lens, bt_mix, False)
    hm_b = _mlstm_dir(proj, small, small_t, mlstm_gate_b, seq_lens, bt_mix, True)
    hg_f = _gla_dir(proj, small, gla_w_lr, gla_b_lr, seq_lens, bt_mix, False)
    hg_b = _gla_dir(proj, small, gla_w_lr, gla_b_lr, seq_lens, bt_mix, True)

    x1, xn, top_idx, gates = _out_proj(
        x, hm_f, hm_b, hg_f, hg_b, proj, mlstm_norm_w.reshape(1, M_WIDTH), gla_norm_w.reshape(1, G_WIDTH),
        w_out.astype(BF16), norm2_w.reshape(1, D_MODEL), router_w.astype(F32),
        router_b.reshape(1, n_e).astype(F32), bm_out)

    dest, buf_tok, block_e, n_used = _route(top_idx, n_e, bm_moe)
    x_sorted = jnp.take(xn, buf_tok, axis=0)
    wg = w_gate_up[:, :, 0::2].astype(BF16)
    wu = w_gate_up[:, :, 1::2].astype(BF16)
    bg = b_gate_up[:, None, 0::2].astype(F32)
    bu = b_gate_up[:, None, 1::2].astype(F32)
    ys = _moe_experts(x_sorted, block_e, n_used, wg, wu, w_down.astype(BF16), bg, bu,
                      b_down[:, None, :].astype(F32), bm_moe, bf_moe)
    dest2 = dest.reshape(t, TOP_K)
    yk = [jnp.take(ys, dest2[:, k], axis=0) for k in range(TOP_K)]
    return _combine(x1, yk, gates, final_norm_w.reshape(1, D_MODEL), bm_comb)


def kernel(x_prompt, x_sample, norm1_w, w_in, mlstm_gate_b, mlstm_norm_w, gla_w_lr, gla_b_lr, gla_norm_w,
           w_out, norm2_w, router_w, router_b, w_gate_up, b_gate_up, w_down, b_down, final_norm_w):
    bp, sp, _ = x_prompt.shape
    bs, ss, _ = x_sample.shape
    x = jnp.concatenate([x_prompt.reshape(bp * sp, D_MODEL), x_sample.reshape(bs * ss, D_MODEL)], axis=0)
    seq_lens = (sp,) * bp + (ss,) * bs
    y = _trunk(x, seq_lens, norm1_w[0], w_in[0], mlstm_gate_b[0], mlstm_norm_w[0], gla_w_lr[0], gla_b_lr[0],
               gla_norm_w[0], w_out[0], norm2_w[0], router_w[0], router_b[0], w_gate_up[0], b_gate_up[0],
               w_down[0], b_down[0], final_norm_w)
    return (y[:bp * sp].reshape(bp, sp, D_MODEL), y[bp * sp:].reshape(bs, ss, D_MODEL))
```

```python
import functools

import jax
import jax.numpy as jnp
from jax import lax
from jax.experimental import pallas as pl
from jax.experimental.pallas import tpu as pltpu

F32 = jnp.float32
BF16 = jnp.bfloat16

D_MODEL = 2048
CHUNK = 64
M_HEADS = 4
M_QK = 128
M_V = 256
M_WIDTH = M_HEADS * M_V
GATE_CAP = 15.0
G_HEADS = 4
G_QK = 128
G_V = 256
G_WIDTH = G_HEADS * G_V
G_RANK = 16
G_TAU = 16.0
TOP_K = 4
SWIGLU_LIMIT = 7.0
SWIGLU_ALPHA = 1.702
EPS = 1e-6
MAIN_WIDTH = 2 * M_HEADS * M_QK + 2 * M_WIDTH + 2 * G_HEADS * G_QK + 2 * G_WIDTH
SMALL_WIDTH = 128
NEG_INF = float("-inf")

VMEM_LIMIT = 48 * 1024 * 1024


def _cparams(sem):
    return pltpu.CompilerParams(dimension_semantics=sem, vmem_limit_bytes=VMEM_LIMIT)


def _log_sigmoid(x):
    return jnp.minimum(x, 0.0) - jnp.log1p(jnp.exp(-jnp.abs(x)))


def _split3(x):
    hi = x.astype(BF16)
    r = x - hi.astype(F32)
    mid = r.astype(BF16)
    lo = (r - mid.astype(F32)).astype(BF16)
    return hi, mid, lo


def _dot(a, b):
    return jnp.dot(a, b, preferred_element_type=F32)


def _dot_nt(a, b):
    return lax.dot_general(a, b, (((1,), (1,)), ((), ())), preferred_element_type=F32)


def _dot_tn(a, b):
    return lax.dot_general(a, b, (((0,), (0,)), ((), ())), preferred_element_type=F32)


def _cum_left(tri, x):
    hi, mid, lo = _split3(x)
    return _dot(tri, hi) + _dot(tri, mid) + _dot(tri, lo)


def _cum_right(x, tri):
    hi, mid, lo = _split3(x)
    return _dot(hi, tri) + _dot(mid, tri) + _dot(lo, tri)


def _inproj_kernel(x_ref, nw_ref, w_ref, ws_ref, wst_ref, main_ref, small_ref, smallt_ref, hn_ref):
    @pl.when(pl.program_id(1) == 0)
    def _():
        x = x_ref[...]
        y = x * lax.rsqrt(jnp.mean(x * x, axis=-1, keepdims=True) + EPS) * nw_ref[...]
        hb = y.astype(BF16)
        hn_ref[...] = hb
        small_ref[...] = _dot(hb, ws_ref[...])
        smallt_ref[...] = _dot_nt(wst_ref[...], hb)

    main_ref[...] = _dot(hn_ref[...], w_ref[...]).astype(BF16)


def _in_proj(x, norm_w, w_main, w_small, w_small_t, bm, bn):
    t = x.shape[0]
    return pl.pallas_call(
        _inproj_kernel,
        grid=(t // bm, MAIN_WIDTH // bn),
        in_specs=[
            pl.BlockSpec((bm, D_MODEL), lambda i, j: (i, 0)),
            pl.BlockSpec((1, D_MODEL), lambda i, j: (0, 0)),
            pl.BlockSpec((D_MODEL, bn), lambda i, j: (0, j)),
            pl.BlockSpec((D_MODEL, SMALL_WIDTH), lambda i, j: (0, 0)),
            pl.BlockSpec((SMALL_WIDTH, D_MODEL), lambda i, j: (0, 0)),
        ],
        out_specs=[
            pl.BlockSpec((bm, bn), lambda i, j: (i, j)),
            pl.BlockSpec((bm, SMALL_WIDTH), lambda i, j: (i, 0)),
            pl.BlockSpec((SMALL_WIDTH, bm), lambda i, j: (0, i)),
        ],
        out_shape=[
            jax.ShapeDtypeStruct((t, MAIN_WIDTH), BF16),
            jax.ShapeDtypeStruct((t, SMALL_WIDTH), F32),
            jax.ShapeDtypeStruct((SMALL_WIDTH, t), F32),
        ],
        scratch_shapes=[pltpu.VMEM((bm, D_MODEL), BF16)],
        compiler_params=_cparams(("arbitrary", "arbitrary")),
        name="in_proj",
    )(x, norm_w, w_main, w_small, w_small_t)


def _chunk_masks(reverse):
    row = lax.broadcasted_iota(jnp.int32, (CHUNK, CHUNK), 0)
    col = lax.broadcasted_iota(jnp.int32, (CHUNK, CHUNK), 1)
    return (col >= row) if reverse else (col <= row)


def _reset_pred(blk, reset_blocks):
    pred = blk == reset_blocks[0]
    for rb in reset_blocks[1:]:
        pred = jnp.logical_or(pred, blk == rb)
    return pred


def _mlstm_kernel(q_ref, k_ref, v_ref, sm_ref, smt_ref, gb_ref, gbt_ref, tri_ref, trit_ref,
                  h_ref, c_ref, n_ref, m_ref, *, reverse, nblocks, reset_blocks, bt):
    i = pl.program_id(0)
    blk = (nblocks - 1 - i) if reverse else i

    @pl.when(_reset_pred(blk, reset_blocks))
    def _():
        c_ref[...] = jnp.zeros_like(c_ref)
        n_ref[...] = jnp.zeros_like(n_ref)
        m_ref[...] = jnp.zeros_like(m_ref)

    d = 1 if reverse else 0
    scale = M_QK ** -0.5
    gates = GATE_CAP * jnp.tanh((sm_ref[...] + gb_ref[...]) / GATE_CAP)
    bcum = _cum_left(tri_ref[...], _log_sigmoid(gates))
    gates_t = GATE_CAP * jnp.tanh((smt_ref[0:16, :] + gbt_ref[...]) / GATE_CAP)
    bcum_t = _cum_right(_log_sigmoid(gates_t), trit_ref[...])
    drow_all = gates_t[8 * d:8 * d + 4, :] - bcum_t[8 * d + 4:8 * d + 8, :]
    mask = _chunk_masks(reverse)
    nchunks = bt // CHUNK
    order = range(nchunks - 1, -1, -1) if reverse else range(nchunks)
    for c in order:
        r0 = c * CHUNK
        for h in range(M_HEADS):
            il = 8 * d + h
            fl = 8 * d + 4 + h
            b = bcum[r0:r0 + CHUNK, fl:fl + 1]
            ic = gates[r0:r0 + CHUNK, il:il + 1]
            g = b[0:1] if reverse else b[CHUNK - 1:CHUNK]
            w = g - b + ic
            m_loc = jnp.max(w, axis=0, keepdims=True)
            e = jnp.exp(w - m_loc)
            qb = q_ref[r0:r0 + CHUNK, h * M_QK:(h + 1) * M_QK]
            kb = k_ref[r0:r0 + CHUNK, h * M_QK:(h + 1) * M_QK]
            vb = v_ref[r0:r0 + CHUNK, h * M_V:(h + 1) * M_V]
            ek = e * kb.astype(F32)
            kv_loc = _dot_tn(ek.astype(BF16), vb)
            n_loc = jnp.sum(ek, axis=0, keepdims=True)

            m_in = m_ref[h]
            c_in = c_ref[h]
            n_in = n_ref[h]
            s = _dot_nt(qb, kb) * scale
            a_intra = jnp.where(mask, b + drow_all[h:h + 1, r0:r0 + CHUNK], NEG_INF)
            a_inter = b + m_in
            m_row = jnp.maximum(jnp.max(a_intra, axis=1, keepdims=True), a_inter)
            p = jnp.exp(a_intra - m_row) * s
            inter = jnp.exp(a_inter - m_row)
            qn = jnp.sum(qb.astype(F32) * n_in, axis=1, keepdims=True) * scale
            den = jnp.sum(p, axis=1, keepdims=True) + inter * qn
            rinv = 1.0 / jnp.maximum(jnp.abs(den), jnp.exp(-m_row))
            qc = _dot(qb, c_in.astype(BF16)) * scale
            hcur = _dot((p * rinv).astype(BF16), vb) + (inter * rinv) * qc
            h_ref[r0:r0 + CHUNK, h * M_V:(h + 1) * M_V] = hcur.astype(h_ref.dtype)

            m_new = jnp.maximum(g + m_in, m_loc)
            aa = jnp.exp(g + m_in - m_new)
            bb = jnp.exp(m_loc - m_new)
            c_ref[h] = aa * c_in + bb * kv_loc
            n_ref[h] = aa * n_in + bb * n_loc
            m_ref[h] = m_new


def _gla_kernel(q_ref, k_ref, v_ref, sm_ref, wlr_ref, blr_ref, tri_ref, o_ref, s_ref,
                *, reverse, nblocks, reset_blocks, bt):
    i = pl.program_id(0)
    blk = (nblocks - 1 - i) if reverse else i

    @pl.when(_reset_pred(blk, reset_blocks))
    def _():
        s_ref[...] = jnp.zeros_like(s_ref)

    scale = G_QK ** -0.5
    z = jnp.dot(sm_ref[...], wlr_ref[...], preferred_element_type=F32,
                precision=lax.Precision.HIGHEST) + blr_ref[...]
    log_a = _log_sigmoid(z) * (1.0 / G_TAU)
    bcum = _cum_left(tri_ref[...], log_a)
    mask = _chunk_masks(reverse)
    nchunks = bt // CHUNK
    order = range(nchunks - 1, -1, -1) if reverse else range(nchunks)
    for c in order:
        r0 = c * CHUNK
        for h in range(G_HEADS):
            b = bcum[r0:r0 + CHUNK, h * G_QK:(h + 1) * G_QK]
            g = b[0:1] if reverse else b[CHUNK - 1:CHUNK]
            qf = q_ref[r0:r0 + CHUNK, h * G_QK:(h + 1) * G_QK].astype(F32)
            kf = k_ref[r0:r0 + CHUNK, h * G_QK:(h + 1) * G_QK].astype(F32)
            vb = v_ref[r0:r0 + CHUNK, h * G_V:(h + 1) * G_V]
            q_dec = (qf * scale * jnp.exp(b)).astype(BF16)
            k_inv = (kf * jnp.exp(-b)).astype(BF16)
            k_end = (kf * jnp.exp(g - b)).astype(BF16)
            st_in = s_ref[h]
            att = jnp.where(mask, _dot_nt(q_dec, k_inv), 0.0)
            o = _dot(att.astype(BF16), vb) + _dot_nt(q_dec, st_in.astype(BF16))
            o_ref[r0:r0 + CHUNK, h * G_V:(h + 1) * G_V] = o.astype(o_ref.dtype)
            s_ref[h] = st_in * jnp.exp(g) + _dot_tn(vb, k_end)


def _block_tri(bt, upper):
    r = jnp.arange(bt)
    same = (r[:, None] // CHUNK) == (r[None, :] // CHUNK)
    tri = (r[None, :] >= r[:, None]) if upper else (r[None, :] <= r[:, None])
    return (same & tri).astype(BF16)


def _reset_blocks(seq_lens, bt, reverse):
    out, off = [], 0
    for s in seq_lens:
        assert s % bt == 0
        out.append((off + s) // bt - 1 if reverse else off // bt)
        off += s
    return tuple(out)


def _mlstm_dir(proj, small, small_t, gate_b, seq_lens, bt, reverse):
    t = proj.shape[0]
    nb = t // bt
    tri = _block_tri(bt, reverse)
    rev = (lambda i: nb - 1 - i) if reverse else (lambda i: i)
    gb = jnp.zeros((1, SMALL_WIDTH), F32).at[0, :16].set(gate_b.reshape(16))
    gbt = gate_b.reshape(16, 1)
    kern = functools.partial(_mlstm_kernel, reverse=reverse, nblocks=nb,
                             reset_blocks=_reset_blocks(seq_lens, bt, reverse), bt=bt)
    return pl.pallas_call(
        kern,
        grid=(nb,),
        in_specs=[
            pl.BlockSpec((bt, 512), lambda i: (rev(i), 0)),
            pl.BlockSpec((bt, 512), lambda i: (rev(i), 1)),
            pl.BlockSpec((bt, 1024), lambda i: (rev(i), 1)),
            pl.BlockSpec((bt, SMALL_WIDTH), lambda i: (rev(i), 0)),
            pl.BlockSpec((SMALL_WIDTH, bt), lambda i: (0, rev(i))),
            pl.BlockSpec((1, SMALL_WIDTH), lambda i: (0, 0)),
            pl.BlockSpec((16, 1), lambda i: (0, 0)),
            pl.BlockSpec((bt, bt), lambda i: (0, 0)),
            pl.BlockSpec((bt, bt), lambda i: (0, 0)),
        ],
        out_specs=pl.BlockSpec((bt, M_WIDTH), lambda i: (rev(i), 0)),
        out_shape=jax.ShapeDtypeStruct((t, M_WIDTH), BF16),
        scratch_shapes=[pltpu.VMEM((M_HEADS, M_QK, M_V), F32),
                        pltpu.VMEM((M_HEADS, 1, M_QK), F32),
                        pltpu.VMEM((M_HEADS, 1, 1), F32)],
        compiler_params=_cparams(("arbitrary",)),
        name="mlstm_bwd" if reverse else "mlstm_fwd",
    )(proj, proj, proj, small, small_t, gb, gbt, tri, tri.T)


def _gla_dir(proj, small, w_lr, b_lr, seq_lens, bt, reverse):
    t = proj.shape[0]
    nb = t // bt
    d = 1 if reverse else 0
    tri = _block_tri(bt, reverse)
    rev = (lambda i: nb - 1 - i) if reverse else (lambda i: i)
    wlr = jnp.zeros((SMALL_WIDTH, G_HEADS * G_QK), F32).at[16 + 16 * d:32 + 16 * d].set(w_lr[d].astype(F32))
    blr = b_lr[d].astype(F32).reshape(1, G_HEADS * G_QK)
    kern = functools.partial(_gla_kernel, reverse=reverse, nblocks=nb,
                             reset_blocks=_reset_blocks(seq_lens, bt, reverse), bt=bt)
    return pl.pallas_call(
        kern,
        grid=(nb,),
        in_specs=[
            pl.BlockSpec((bt, 512), lambda i: (rev(i), 6)),
            pl.BlockSpec((bt, 512), lambda i: (rev(i), 7)),
            pl.BlockSpec((bt, 1024), lambda i: (rev(i), 4)),
            pl.BlockSpec((bt, SMALL_WIDTH), lambda i: (rev(i), 0)),
            pl.BlockSpec((SMALL_WIDTH, G_HEADS * G_QK), lambda i: (0, 0)),
            pl.BlockSpec((1, G_HEADS * G_QK), lambda i: (0, 0)),
            pl.BlockSpec((bt, bt), lambda i: (0, 0)),
        ],
        out_specs=pl.BlockSpec((bt, G_WIDTH), lambda i: (rev(i), 0)),
        out_shape=jax.ShapeDtypeStruct((t, G_WIDTH), BF16),
        scratch_shapes=[pltpu.VMEM((G_HEADS, G_V, G_QK), F32)],
        compiler_params=_cparams(("arbitrary",)),
        name="gla_bwd" if reverse else "gla_fwd",
    )(proj, proj, proj, small, wlr, blr, tri)


def _head_norm(hsum, w, nheads, hd):
    outs = []
    for h in range(nheads):
        x = hsum[:, h * hd:(h + 1) * hd]
        outs.append(x * lax.rsqrt(jnp.mean(x * x, axis=-1, keepdims=True) + EPS))
    return jnp.concatenate(outs, axis=-1) * w


def _outproj_kernel(x_ref, mf_ref, mb_ref, gf_ref, gb_ref, mo_ref, gr_ref, mnw_ref, gnw_ref, wout_ref,
                    n2w_ref, rw_ref, rb_ref, x1_ref, xn_ref, idx_ref, gate_ref, *, n_experts):
    hm = mf_ref[...].astype(F32) + mb_ref[...].astype(F32)
    a_out = _head_norm(hm, mnw_ref[...], M_HEADS, M_V) * jax.nn.sigmoid(mo_ref[...].astype(F32))
    hg = gf_ref[...].astype(F32) + gb_ref[...].astype(F32)
    r = gr_ref[...].astype(F32)
    b_out = _head_norm(hg, gnw_ref[...], G_HEADS, G_V) * (r * jax.nn.sigmoid(r))
    mixed = jnp.concatenate([a_out, b_out], axis=-1).astype(BF16)
    x1 = x_ref[...] + _dot(mixed, wout_ref[...])
    x1_ref[...] = x1
    xn = x1 * lax.rsqrt(jnp.mean(x1 * x1, axis=-1, keepdims=True) + EPS) * n2w_ref[...]
    xn_ref[...] = xn
    logits = jnp.dot(xn, rw_ref[...], preferred_element_type=F32,
                     precision=lax.Precision.HIGHEST) + rb_ref[...]
    lane = lax.broadcasted_iota(jnp.int32, logits.shape, 1).astype(F32)
    vals, idxs = [], []
    cur = logits
    for _ in range(TOP_K):
        m = jnp.max(cur, axis=1, keepdims=True)
        sel = jnp.min(jnp.where(cur == m, lane, float(n_experts)), axis=1, keepdims=True)
        vals.append(m)
        idxs.append(sel)
        cur = jnp.where(lane == sel, NEG_INF, cur)
    exps = [jnp.exp(v - vals[0]) for v in vals]
    tot = exps[0] + exps[1] + exps[2] + exps[3]
    gate_ref[...] = jnp.concatenate(exps, axis=1) / tot
    idx_ref[...] = jnp.concatenate(idxs, axis=1).astype(jnp.int32)


def _out_proj(x, hm_f, hm_b, hg_f, hg_b, proj, mnw, gnw, w_out, n2w, router_w, router_b, bm):
    t = x.shape[0]
    n_e = router_w.shape[1]
    row = lambda c: pl.BlockSpec((bm, 1024), lambda i: (i, c))
    full = lambda a: pl.BlockSpec(a.shape, lambda i: (0,) * a.ndim)
    args = (x, hm_f, hm_b, hg_f, hg_b, proj, proj, mnw, gnw, w_out, n2w, router_w, router_b)
    return pl.pallas_call(
        functools.partial(_outproj_kernel, n_experts=n_e),
        grid=(t // bm,),
        in_specs=[pl.BlockSpec((bm, D_MODEL), lambda i: (i, 0)), row(0), row(0), row(0), row(0),
                  row(2), row(5)] + [full(a) for a in args[7:]],
        out_specs=[pl.BlockSpec((bm, D_MODEL), lambda i: (i, 0)),
                   pl.BlockSpec((bm, D_MODEL), lambda i: (i, 0)),
                   pl.BlockSpec((bm, TOP_K), lambda i: (i, 0)),
                   pl.BlockSpec((bm, TOP_K), lambda i: (i, 0))],
        out_shape=[jax.ShapeDtypeStruct((t, D_MODEL), F32),
                   jax.ShapeDtypeStruct((t, D_MODEL), F32),
                   jax.ShapeDtypeStruct((t, TOP_K), jnp.int32),
                   jax.ShapeDtypeStruct((t, TOP_K), F32)],
        compiler_params=_cparams(("arbitrary",)),
        name="out_proj_router",
    )(*args)


DEINT = 512


def _deinterleave_kernel(w_ref, perm_ref, out_ref, *, bf):
    half = DEINT // 2
    for c in range(2 * bf // DEINT):
        w16 = w_ref[0, :, c * DEINT:(c + 1) * DEINT].astype(BF16)
        res = _dot(w16, perm_ref[...]).astype(BF16)
        out_ref[0, 0, :, c * half:(c + 1) * half] = res[:, :half]
        out_ref[0, 0, :, bf + c * half:bf + (c + 1) * half] = res[:, half:]


def _deinterleave_gate_up(w_gate_up, bf):
    n_e, d, two_f = w_gate_up.shape
    nf = two_f // (2 * bf)
    r = jnp.arange(DEINT)[:, None]
    c = jnp.arange(DEINT)[None, :]
    perm = jnp.where(c < DEINT // 2, r == 2 * c, r == 2 * (c - DEINT // 2) + 1).astype(BF16)
    return pl.pallas_call(
        functools.partial(_deinterleave_kernel, bf=bf),
        grid=(n_e, nf),
        in_specs=[pl.BlockSpec((1, d, 2 * bf), lambda e, f: (e, 0, f)),
                  pl.BlockSpec((DEINT, DEINT), lambda e, f: (0, 0))],
        out_specs=pl.BlockSpec((1, 1, d, 2 * bf), lambda e, f: (e, f, 0, 0)),
        out_shape=jax.ShapeDtypeStruct((n_e, nf, d, 2 * bf), BF16),
        compiler_params=_cparams(("arbitrary", "arbitrary")),
        name="deinterleave_gate_up",
    )(w_gate_up, perm)


def _moe_kernel(be_ref, nu_ref, x_ref, wgu_ref, wd_ref, bgu_ref, bd_ref, out_ref, *, bf):
    i = pl.program_id(0)
    f = pl.program_id(1)
    used = i < nu_ref[0]

    @pl.when(jnp.logical_and(jnp.logical_not(used), f == 0))
    def _():
        out_ref[...] = jnp.zeros_like(out_ref)

    @pl.when(used)
    def _():
        gu = _dot(x_ref[...].astype(BF16), wgu_ref[0, 0]) + bgu_ref[0, 0]
        g = jnp.minimum(gu[:, :bf], SWIGLU_LIMIT)
        u = jnp.clip(gu[:, bf:], -SWIGLU_LIMIT, SWIGLU_LIMIT)
        act = ((u + 1.0) * g * jax.nn.sigmoid(g * SWIGLU_ALPHA)).astype(BF16)
        contrib = _dot(act, wd_ref[0])

        @pl.when(f == 0)
        def _():
            out_ref[...] = contrib + bd_ref[0]

        @pl.when(f != 0)
        def _():
            out_ref[...] += contrib


def _moe_experts(x_sorted, block_e, n_used, wgu, wd, bgu, bd, bm):
    rows = x_sorted.shape[0]
    nblocks = rows // bm
    nf, bf = wgu.shape[1], wgu.shape[3] // 2

    def fe(i, f, be, nu):
        return jnp.where(i < nu[0], f, nf - 1)

    grid_spec = pltpu.PrefetchScalarGridSpec(
        num_scalar_prefetch=2,
        grid=(nblocks, nf),
        in_specs=[
            pl.BlockSpec((bm, D_MODEL), lambda i, f, be, nu: (jnp.minimum(i, nu[0] - 1), 0)),
            pl.BlockSpec((1, 1, D_MODEL, 2 * bf), lambda i, f, be, nu: (be[i], fe(i, f, be, nu), 0, 0)),
            pl.BlockSpec((1, bf, D_MODEL), lambda i, f, be, nu: (be[i], fe(i, f, be, nu), 0)),
            pl.BlockSpec((1, 1, 1, 2 * bf), lambda i, f, be, nu: (be[i], fe(i, f, be, nu), 0, 0)),
            pl.BlockSpec((1, 1, D_MODEL), lambda i, f, be, nu: (be[i], 0, 0)),
        ],
        out_specs=pl.BlockSpec((bm, D_MODEL), lambda i, f, be, nu: (i, 0)),
    )
    return pl.pallas_call(
        functools.partial(_moe_kernel, bf=bf),
        grid_spec=grid_spec,
        out_shape=jax.ShapeDtypeStruct((rows, D_MODEL), F32),
        compiler_params=_cparams(("arbitrary", "arbitrary")),
        name="moe_experts",
    )(block_e, n_used, x_sorted, wgu, wd, bgu, bd)


def _row_copy(src_ref, src_row, dst_ref, dst_row, sem):
    return pltpu.make_async_copy(src_ref.at[pl.ds(src_row, 1), :], dst_ref.at[pl.ds(dst_row, 1), :], sem)


def _dispatch_kernel(pe_ref, dest_ref, xn_ref, xs_ref, zero_ref, sem, zsem, *, bm, bm_moe, n_experts):
    @pl.when(pl.program_id(0) == 0)
    def _():
        zero_ref[...] = jnp.zeros_like(zero_ref)

        def last_block(e):
            start = pl.multiple_of(pe_ref[e + 1] - bm_moe, bm_moe)
            return pltpu.make_async_copy(zero_ref, xs_ref.at[pl.ds(start, bm_moe), :], zsem)

        for e in range(n_experts):
            @pl.when(pe_ref[e + 1] > pe_ref[e])
            def _():
                last_block(e).start()
        for e in range(n_experts):
            @pl.when(pe_ref[e + 1] > pe_ref[e])
            def _():
                last_block(e).wait()

        def unused_block(b, carry):
            cp = pltpu.make_async_copy(zero_ref, xs_ref.at[pl.ds(pl.multiple_of(b * bm_moe, bm_moe), bm_moe), :], zsem)
            cp.start()
            cp.wait()
            return carry
        lax.fori_loop(pe_ref[n_experts] // bm_moe, xs_ref.shape[0] // bm_moe, unused_block, 0)

    def issue(r, carry):
        for k in range(TOP_K):
            _row_copy(xn_ref, r, xs_ref, dest_ref[TOP_K * r + k], sem).start()
        return carry

    def drain(r, carry):
        for k in range(TOP_K):
            _row_copy(xn_ref, r, xs_ref, dest_ref[TOP_K * r + k], sem).wait()
        return carry

    lax.fori_loop(0, bm, issue, 0, unroll=8)
    lax.fori_loop(0, bm, drain, 0, unroll=8)


def _dispatch(xn, dest, pad_edges, rows, bm, bm_moe):
    t = xn.shape[0]
    n_e = pad_edges.shape[0] - 1
    grid_spec = pltpu.PrefetchScalarGridSpec(
        num_scalar_prefetch=1,
        grid=(t // bm,),
        in_specs=[pl.BlockSpec((bm * TOP_K,), lambda i, pe: (i,), memory_space=pltpu.SMEM),
                  pl.BlockSpec((bm, D_MODEL), lambda i, pe: (i, 0))],
        out_specs=pl.BlockSpec(memory_space=pl.ANY),
        scratch_shapes=[pltpu.VMEM((bm_moe, D_MODEL), F32),
                        pltpu.SemaphoreType.DMA(()), pltpu.SemaphoreType.DMA(())],
    )
    return pl.pallas_call(
        functools.partial(_dispatch_kernel, bm=bm, bm_moe=bm_moe, n_experts=n_e),
        grid_spec=grid_spec,
        out_shape=jax.ShapeDtypeStruct((rows, D_MODEL), F32),
        compiler_params=_cparams(("arbitrary",)),
        name="dispatch_rows",
    )(pad_edges, dest, xn)


def _combine_kernel(dcur_ref, dnxt_ref, x1_ref, gate_ref, fw_ref, ys_ref, out_ref, ybuf, sems, *, bm):
    i = pl.program_id(0)
    nb = pl.num_programs(0)
    slot = lax.rem(i, 2)

    def issue(dref, s):
        def body(r, carry):
            for k in range(TOP_K):
                _row_copy(ys_ref, dref[TOP_K * r + k], ybuf.at[s, k], r, sems.at[s]).start()
            return carry
        lax.fori_loop(0, bm, body, 0, unroll=8)

    @pl.when(i == 0)
    def _():
        issue(dcur_ref, 0)

    @pl.when(i + 1 < nb)
    def _():
        issue(dnxt_ref, 1 - slot)

    def drain(r, carry):
        for k in range(TOP_K):
            _row_copy(ys_ref, dcur_ref[TOP_K * r + k], ybuf.at[slot, k], r, sems.at[slot]).wait()
        return carry
    lax.fori_loop(0, bm, drain, 0, unroll=8)

    gate = gate_ref[...]
    y = (ybuf[slot, 0] * gate[:, 0:1] + ybuf[slot, 1] * gate[:, 1:2]
         + ybuf[slot, 2] * gate[:, 2:3] + ybuf[slot, 3] * gate[:, 3:4])
    x2 = x1_ref[...] + y
    out_ref[...] = x2 * lax.rsqrt(jnp.mean(x2 * x2, axis=-1, keepdims=True) + EPS) * fw_ref[...]


def _combine(x1, ys, dest, gates, final_w, bm):
    t = x1.shape[0]
    nb = t // bm
    blk = pl.BlockSpec((bm, D_MODEL), lambda i: (i, 0))
    return pl.pallas_call(
        functools.partial(_combine_kernel, bm=bm),
        grid=(nb,),
        in_specs=[pl.BlockSpec((bm * TOP_K,), lambda i: (i,), memory_space=pltpu.SMEM),
                  pl.BlockSpec((bm * TOP_K,), lambda i: (jnp.minimum(i + 1, nb - 1),), memory_space=pltpu.SMEM),
                  blk,
                  pl.BlockSpec((bm, TOP_K), lambda i: (i, 0)),
                  pl.BlockSpec((1, D_MODEL), lambda i: (0, 0)),
                  pl.BlockSpec(memory_space=pl.ANY)],
        out_specs=blk,
        out_shape=jax.ShapeDtypeStruct((t, D_MODEL), F32),
        scratch_shapes=[pltpu.VMEM((2, TOP_K, bm, D_MODEL), F32), pltpu.SemaphoreType.DMA((2,))],
        compiler_params=_cparams(("arbitrary",)),
        name="combine_final_norm",
    )(dest, dest, x1, gates, final_w, ys)


def _route(top_idx, n_experts, bm):
    t = top_idx.shape[0]
    n_assign = t * TOP_K
    flat_e = top_idx.reshape(n_assign)
    onehot = (flat_e[:, None] == jnp.arange(n_experts, dtype=jnp.int32)[None, :]).astype(jnp.int32)
    csum = jnp.cumsum(onehot, axis=0)
    rank = jnp.sum((csum - onehot) * onehot, axis=1)
    counts = csum[-1]
    padded = ((counts + bm - 1) // bm) * bm
    pad_end = jnp.cumsum(padded)
    pad_start = pad_end - padded
    dest = (pad_start[flat_e] + rank).astype(jnp.int32)
    nblocks = n_assign // bm + n_experts
    n_used = (pad_end[-1] // bm).astype(jnp.int32).reshape(1)
    block_start = jnp.arange(nblocks, dtype=jnp.int32) * bm
    block_e = jnp.minimum(jnp.searchsorted(pad_end, block_start, side="right"), n_experts - 1)
    last_e = block_e[jnp.maximum(n_used[0] - 1, 0)]
    block_e = jnp.where(jnp.arange(nblocks) < n_used[0], block_e, last_e).astype(jnp.int32)
    pad_edges = jnp.concatenate([jnp.zeros((1,), jnp.int32), pad_end.astype(jnp.int32)])
    return dest, pad_edges, block_e, n_used, nblocks * bm


def _trunk(x, seq_lens, norm1_w, w_in, mlstm_gate_b, mlstm_norm_w, gla_w_lr, gla_b_lr, gla_norm_w, w_out,
           norm2_w, router_w, router_b, w_gate_up, b_gate_up, w_down, b_down, final_norm_w,
           *, bm_in=512, bn_in=1024, bt_mix=256, bm_out=256, bm_disp=256, bm_moe=512, bf_moe=512, bm_comb=256):
    t = x.shape[0]
    n_e = router_w.shape[1]
    w_main = jnp.concatenate([w_in[:, :3072], w_in[:, 3088:6160]], axis=1).astype(BF16)
    w_small = jnp.zeros((D_MODEL, SMALL_WIDTH), F32)
    w_small = w_small.at[:, 0:16].set(w_in[:, 3072:3088]).at[:, 16:48].set(w_in[:, 6160:6192]).astype(BF16)
    proj, small, small_t = _in_proj(x, norm1_w.reshape(1, D_MODEL), w_main, w_small, w_small.T, bm_in, bn_in)

    hm_f = _mlstm_dir(proj, small, small_t, mlstm_gate_b, seq_lens, bt_mix, False)
    hm_b = _mlstm_dir(proj, small, small_t, mlstm_gate_b, seq_lens, bt_mix, True)
    hg_f = _gla_dir(proj, small, gla_w_lr, gla_b_lr, seq_lens, bt_mix, False)
    hg_b = _gla_dir(proj, small, gla_w_lr, gla_b_lr, seq_lens, bt_mix, True)

    x1, xn, top_idx, gates = _out_proj(
        x, hm_f, hm_b, hg_f, hg_b, proj, mlstm_norm_w.reshape(1, M_WIDTH), gla_norm_w.reshape(1, G_WIDTH),
        w_out.astype(BF16), norm2_w.reshape(1, D_MODEL), router_w.astype(F32),
        router_b.reshape(1, n_e).astype(F32), bm_out)

    dest, pad_edges, block_e, n_used, rows = _route(top_idx, n_e, bm_moe)
    x_sorted = _dispatch(xn, dest, pad_edges, rows, bm_disp, bm_moe)
    nf = w_gate_up.shape[2] // (2 * bf_moe)
    wgu = _deinterleave_gate_up(w_gate_up, bf_moe)
    bgu = jnp.concatenate([b_gate_up[:, 0::2].reshape(n_e, nf, 1, bf_moe),
                           b_gate_up[:, 1::2].reshape(n_e, nf, 1, bf_moe)], axis=-1).astype(F32)
    ys = _moe_experts(x_sorted, block_e, n_used, wgu, w_down.astype(BF16), bgu,
                      b_down[:, None, :].astype(F32), bm_moe)
    return _combine(x1, ys, dest, gates, final_norm_w.reshape(1, D_MODEL), bm_comb)


def kernel(x_prompt, x_sample, norm1_w, w_in, mlstm_gate_b, mlstm_norm_w, gla_w_lr, gla_b_lr, gla_norm_w,
           w_out, norm2_w, router_w, router_b, w_gate_up, b_gate_up, w_down, b_down, final_norm_w):
    bp, sp, _ = x_prompt.shape
    bs, ss, _ = x_sample.shape
    x = jnp.concatenate([x_prompt.reshape(bp * sp, D_MODEL), x_sample.reshape(bs * ss, D_MODEL)], axis=0)
    seq_lens = (sp,) * bp + (ss,) * bs
    y = _trunk(x, seq_lens, norm1_w[0], w_in[0], mlstm_gate_b[0], mlstm_norm_w[0], gla_w_lr[0], gla_b_lr[0],
               gla_norm_w[0], w_out[0], norm2_w[0], router_w[0], router_b[0], w_gate_up[0], b_gate_up[0],
               w_down[0], b_down[0], final_norm_w)
    return (y[:bp * sp].reshape(bp, sp, D_MODEL), y[bp * sp:].reshape(bs, ss, D_MODEL))
```

```python
import functools

import jax
import jax.numpy as jnp
from jax import lax
from jax.experimental import pallas as pl
from jax.experimental.pallas import tpu as pltpu

F32 = jnp.float32
BF16 = jnp.bfloat16

D_MODEL = 2048
CHUNK = 64
M_HEADS = 4
M_QK = 128
M_V = 256
M_WIDTH = M_HEADS * M_V
GATE_CAP = 15.0
G_HEADS = 4
G_QK = 128
G_V = 256
G_WIDTH = G_HEADS * G_V
G_RANK = 16
G_TAU = 16.0
TOP_K = 4
SWIGLU_LIMIT = 7.0
SWIGLU_ALPHA = 1.702
EPS = 1e-6
MAIN_WIDTH = 2 * M_HEADS * M_QK + 2 * M_WIDTH + 2 * G_HEADS * G_QK + 2 * G_WIDTH
SMALL_WIDTH = 128
NEG_INF = float("-inf")

VMEM_LIMIT = 48 * 1024 * 1024


def _cparams(sem):
    return pltpu.CompilerParams(dimension_semantics=sem, vmem_limit_bytes=VMEM_LIMIT)


def _log_sigmoid(x):
    return jnp.minimum(x, 0.0) - jnp.log1p(jnp.exp(-jnp.abs(x)))


def _split3(x):
    hi = x.astype(BF16)
    r = x - hi.astype(F32)
    mid = r.astype(BF16)
    lo = (r - mid.astype(F32)).astype(BF16)
    return hi, mid, lo


def _dot(a, b):
    return jnp.dot(a, b, preferred_element_type=F32)


def _dot_nt(a, b):
    return lax.dot_general(a, b, (((1,), (1,)), ((), ())), preferred_element_type=F32)


def _dot_tn(a, b):
    return lax.dot_general(a, b, (((0,), (0,)), ((), ())), preferred_element_type=F32)


def _cum_left(tri, x):
    hi, mid, lo = _split3(x)
    return _dot(tri, hi) + _dot(tri, mid) + _dot(tri, lo)


def _cum_right(x, tri):
    hi, mid, lo = _split3(x)
    return _dot(hi, tri) + _dot(mid, tri) + _dot(lo, tri)


def _inproj_kernel(xa_ref, xb_ref, nw_ref, w_ref, ws_ref, wst_ref, wkt_ref, main_ref, small_ref, smallt_ref, kt_ref,
                   hn_ref, *, nb_a):
    @pl.when(pl.program_id(1) == 0)
    def _():
        x = jnp.where(pl.program_id(0) < nb_a, xa_ref[...], xb_ref[...])
        y = x * lax.rsqrt(jnp.mean(x * x, axis=-1, keepdims=True) + EPS) * nw_ref[...]
        hb = y.astype(BF16)
        hn_ref[...] = hb
        small_ref[...] = _dot(hb, ws_ref[...])
        smallt_ref[...] = _dot_nt(wst_ref[...], hb)
        kt_ref[...] = _dot_nt(wkt_ref[...], hb).astype(BF16)

    main_ref[...] = _dot(hn_ref[...], w_ref[...]).astype(BF16)


def _row_specs(bm, nb_a, width):
    return [pl.BlockSpec((bm, width), lambda i, *_: (jnp.minimum(i, nb_a - 1), 0)),
            pl.BlockSpec((bm, width), lambda i, *_: (jnp.maximum(i - nb_a, 0), 0))]


def _in_proj(xa, xb, norm_w, w_main, w_small, w_small_t, w_kt, bm, bn):
    t = xa.shape[0] + xb.shape[0]
    nb_a = xa.shape[0] // bm
    return pl.pallas_call(
        functools.partial(_inproj_kernel, nb_a=nb_a),
        grid=(t // bm, MAIN_WIDTH // bn),
        in_specs=_row_specs(bm, nb_a, D_MODEL) + [
            pl.BlockSpec((1, D_MODEL), lambda i, j: (0, 0)),
            pl.BlockSpec((D_MODEL, bn), lambda i, j: (0, j)),
            pl.BlockSpec((D_MODEL, SMALL_WIDTH), lambda i, j: (0, 0)),
            pl.BlockSpec((SMALL_WIDTH, D_MODEL), lambda i, j: (0, 0)),
            pl.BlockSpec((M_HEADS * M_QK, D_MODEL), lambda i, j: (0, 0)),
        ],
        out_specs=[
            pl.BlockSpec((bm, bn), lambda i, j: (i, j)),
            pl.BlockSpec((bm, SMALL_WIDTH), lambda i, j: (i, 0)),
            pl.BlockSpec((SMALL_WIDTH, bm), lambda i, j: (0, i)),
            pl.BlockSpec((M_HEADS * M_QK, bm), lambda i, j: (0, i)),
        ],
        out_shape=[
            jax.ShapeDtypeStruct((t, MAIN_WIDTH), BF16),
            jax.ShapeDtypeStruct((t, SMALL_WIDTH), F32),
            jax.ShapeDtypeStruct((SMALL_WIDTH, t), F32),
            jax.ShapeDtypeStruct((M_HEADS * M_QK, t), BF16),
        ],
        scratch_shapes=[pltpu.VMEM((bm, D_MODEL), BF16)],
        compiler_params=_cparams(("arbitrary", "arbitrary")),
        name="in_proj",
    )(xa, xb, norm_w, w_main, w_small, w_small_t, w_kt)


def _chunk_masks(reverse):
    row = lax.broadcasted_iota(jnp.int32, (CHUNK, CHUNK), 0)
    col = lax.broadcasted_iota(jnp.int32, (CHUNK, CHUNK), 1)
    return (col >= row) if reverse else (col <= row)


def _reset_pred(blk, reset_blocks):
    pred = blk == reset_blocks[0]
    for rb in reset_blocks[1:]:
        pred = jnp.logical_or(pred, blk == rb)
    return pred


def _mlstm_kernel(q_ref, k_ref, kt_ref, v_ref, sm_ref, smt_ref, gb_ref, gbt_ref, tri_ref, trit_ref,
                  h_ref, c_ref, n_ref, m_ref, *, reverse, nblocks, reset_blocks, bt):
    i = pl.program_id(0)
    blk = (nblocks - 1 - i) if reverse else i

    @pl.when(_reset_pred(blk, reset_blocks))
    def _():
        c_ref[...] = jnp.zeros_like(c_ref)
        n_ref[...] = jnp.zeros_like(n_ref)
        m_ref[...] = jnp.zeros_like(m_ref)

    d = 1 if reverse else 0
    scale = M_QK ** -0.5
    gates = GATE_CAP * jnp.tanh((sm_ref[...] + gb_ref[...]) / GATE_CAP)
    bcum = _cum_left(tri_ref[...], _log_sigmoid(gates))
    gates_t = GATE_CAP * jnp.tanh((smt_ref[0:16, :] + gbt_ref[...]) / GATE_CAP)
    bcum_t = _cum_right(_log_sigmoid(gates_t), trit_ref[...])
    i_row = gates_t[8 * d:8 * d + 4, :]
    b_row = bcum_t[8 * d + 4:8 * d + 8, :]
    last = 0 if reverse else bt - 1
    g4 = b_row[:, last:last + 1]
    w4 = g4 - b_row + i_row
    mloc4 = jnp.max(w4, axis=1, keepdims=True)
    e4 = jnp.exp(w4 - mloc4)
    d4 = i_row - b_row
    row = lax.broadcasted_iota(jnp.int32, (bt, bt), 0)
    col = lax.broadcasted_iota(jnp.int32, (bt, bt), 1)
    mask = (col >= row) if reverse else (col <= row)
    ones_b = jnp.ones((bt, 128), BF16)
    e4b = e4.astype(BF16)
    outs = []
    for h in range(M_HEADS):
        fl = 8 * d + 4 + h
        g, m_loc = g4[h:h + 1], mloc4[h:h + 1]
        qb = q_ref[:, h * M_QK:(h + 1) * M_QK]
        kb = k_ref[:, h * M_QK:(h + 1) * M_QK]
        ktb = kt_ref[h * M_QK:(h + 1) * M_QK, :]
        vb = v_ref[:, h * M_V:(h + 1) * M_V]
        ekt = (ktb.astype(F32) * e4[h:h + 1]).astype(BF16)
        kv_loc = _dot(ekt, vb)
        n_loc = _dot(e4b, kb)[h:h + 1]
        s = _dot(qb, ktb) * scale
        b_b = jnp.broadcast_to(bcum[:, fl:fl + 1], (bt, 128))
        a_intra = jnp.where(mask, jnp.tile(b_b, (1, bt // 128)) + d4[h:h + 1], NEG_INF)
        m_intra = jnp.max(a_intra, axis=1, keepdims=True)
        pb = (jnp.exp(a_intra - m_intra) * s).astype(BF16)
        pv = _dot(pb, vb)
        p_sum = _dot(pb, ones_b)

        m_in, c_in, n_in = m_ref[h], c_ref[h], n_ref[h]
        m_intra_b = jnp.broadcast_to(m_intra, (bt, 128))
        a_inter = b_b + m_in
        m_row = jnp.maximum(m_intra_b, a_inter)
        f_intra = jnp.exp(m_intra_b - m_row)
        f_inter = jnp.exp(a_inter - m_row)
        qn = _dot_nt(qb, jnp.broadcast_to(n_in, (128, M_QK)).astype(BF16)) * scale
        den = f_intra * p_sum + f_inter * qn
        rinv = 1.0 / jnp.maximum(jnp.abs(den), jnp.exp(-m_row))
        qc = _dot(qb, c_in.astype(BF16)) * scale
        w_intra = jnp.tile(f_intra * rinv, (1, M_V // 128))
        w_inter = jnp.tile(f_inter * rinv, (1, M_V // 128))
        outs.append((w_intra * pv + w_inter * qc).astype(h_ref.dtype))

        m_new = jnp.maximum(g + m_in, m_loc)
        aa = jnp.exp(g + m_in - m_new)
        bb = jnp.exp(m_loc - m_new)
        c_ref[h] = aa * c_in + bb * kv_loc
        n_ref[h] = aa * n_in + bb * n_loc
        m_ref[h] = m_new
    h_ref[...] = jnp.concatenate(outs, axis=1)


def _gla_kernel(q_ref, k_ref, v_ref, sm_ref, wlr_ref, blr_ref, tri_ref, o_ref, s_ref,
                *, reverse, nblocks, reset_blocks, bt):
    i = pl.program_id(0)
    blk = (nblocks - 1 - i) if reverse else i

    @pl.when(_reset_pred(blk, reset_blocks))
    def _():
        s_ref[...] = jnp.zeros_like(s_ref)

    scale = G_QK ** -0.5
    z = jnp.dot(sm_ref[...], wlr_ref[...], preferred_element_type=F32,
                precision=lax.Precision.HIGHEST) + blr_ref[...]
    log_a = _log_sigmoid(z) * (1.0 / G_TAU)
    bcum = _cum_left(tri_ref[...], log_a)
    mask = _chunk_masks(reverse)
    nchunks = bt // CHUNK
    order = range(nchunks - 1, -1, -1) if reverse else range(nchunks)
    sts = [s_ref[h] for h in range(G_HEADS)]
    for c in order:
        r0 = c * CHUNK
        outs = []
        for h in range(G_HEADS):
            b = bcum[r0:r0 + CHUNK, h * G_QK:(h + 1) * G_QK]
            g = b[0:1] if reverse else b[CHUNK - 1:CHUNK]
            qf = q_ref[r0:r0 + CHUNK, h * G_QK:(h + 1) * G_QK].astype(F32)
            kf = k_ref[r0:r0 + CHUNK, h * G_QK:(h + 1) * G_QK].astype(F32)
            vb = v_ref[r0:r0 + CHUNK, h * G_V:(h + 1) * G_V]
            q_dec = (qf * scale * jnp.exp(b)).astype(BF16)
            k_inv = (kf * jnp.exp(-b)).astype(BF16)
            k_end = (kf * jnp.exp(g - b)).astype(BF16)
            att = jnp.where(mask, _dot_nt(q_dec, k_inv), 0.0)
            o = _dot(att.astype(BF16), vb) + _dot_nt(q_dec, sts[h].astype(BF16))
            outs.append(o.astype(o_ref.dtype))
            sts[h] = sts[h] * jnp.exp(g) + _dot_tn(vb, k_end)
        o_ref[r0:r0 + CHUNK, :] = jnp.concatenate(outs, axis=1)
    for h in range(G_HEADS):
        s_ref[h] = sts[h]


def _block_tri(bt, upper, chunk=CHUNK):
    r = jnp.arange(bt)
    same = (r[:, None] // chunk) == (r[None, :] // chunk)
    tri = (r[None, :] >= r[:, None]) if upper else (r[None, :] <= r[:, None])
    return (same & tri).astype(BF16)


def _reset_blocks(seq_lens, bt, reverse):
    out, off = [], 0
    for s in seq_lens:
        assert s % bt == 0
        out.append((off + s) // bt - 1 if reverse else off // bt)
        off += s
    return tuple(out)


def _mlstm_dir(proj, k_t, small, small_t, gate_b, seq_lens, bt, reverse):
    t = proj.shape[0]
    nb = t // bt
    tri = _block_tri(bt, reverse, chunk=bt)
    rev = (lambda i: nb - 1 - i) if reverse else (lambda i: i)
    gb = jnp.zeros((1, SMALL_WIDTH), F32).at[0, :16].set(gate_b.reshape(16))
    gbt = gate_b.reshape(16, 1)
    kern = functools.partial(_mlstm_kernel, reverse=reverse, nblocks=nb,
                             reset_blocks=_reset_blocks(seq_lens, bt, reverse), bt=bt)
    return pl.pallas_call(
        kern,
        grid=(nb,),
        in_specs=[
            pl.BlockSpec((bt, 512), lambda i: (rev(i), 0)),
            pl.BlockSpec((bt, 512), lambda i: (rev(i), 1)),
            pl.BlockSpec((M_HEADS * M_QK, bt), lambda i: (0, rev(i))),
            pl.BlockSpec((bt, 1024), lambda i: (rev(i), 1)),
            pl.BlockSpec((bt, SMALL_WIDTH), lambda i: (rev(i), 0)),
            pl.BlockSpec((SMALL_WIDTH, bt), lambda i: (0, rev(i))),
            pl.BlockSpec((1, SMALL_WIDTH), lambda i: (0, 0)),
            pl.BlockSpec((16, 1), lambda i: (0, 0)),
            pl.BlockSpec((bt, bt), lambda i: (0, 0)),
            pl.BlockSpec((bt, bt), lambda i: (0, 0)),
        ],
        out_specs=pl.BlockSpec((bt, M_WIDTH), lambda i: (rev(i), 0)),
        out_shape=jax.ShapeDtypeStruct((t, M_WIDTH), BF16),
        scratch_shapes=[pltpu.VMEM((M_HEADS, M_QK, M_V), F32),
                        pltpu.VMEM((M_HEADS, 1, M_QK), F32),
                        pltpu.VMEM((M_HEADS, 1, 1), F32)],
        compiler_params=_cparams(("arbitrary",)),
        name="mlstm_bwd" if reverse else "mlstm_fwd",
    )(proj, proj, k_t, proj, small, small_t, gb, gbt, tri, tri.T)


def _gla_dir(proj, small, w_lr, b_lr, seq_lens, bt, reverse):
    t = proj.shape[0]
    nb = t // bt
    d = 1 if reverse else 0
    tri = _block_tri(bt, reverse)
    rev = (lambda i: nb - 1 - i) if reverse else (lambda i: i)
    wlr = jnp.zeros((SMALL_WIDTH, G_HEADS * G_QK), F32).at[16 + 16 * d:32 + 16 * d].set(w_lr[d].astype(F32))
    blr = b_lr[d].astype(F32).reshape(1, G_HEADS * G_QK)
    kern = functools.partial(_gla_kernel, reverse=reverse, nblocks=nb,
                             reset_blocks=_reset_blocks(seq_lens, bt, reverse), bt=bt)
    return pl.pallas_call(
        kern,
        grid=(nb,),
        in_specs=[
            pl.BlockSpec((bt, 512), lambda i: (rev(i), 6)),
            pl.BlockSpec((bt, 512), lambda i: (rev(i), 7)),
            pl.BlockSpec((bt, 1024), lambda i: (rev(i), 4)),
            pl.BlockSpec((bt, SMALL_WIDTH), lambda i: (rev(i), 0)),
            pl.BlockSpec((SMALL_WIDTH, G_HEADS * G_QK), lambda i: (0, 0)),
            pl.BlockSpec((1, G_HEADS * G_QK), lambda i: (0, 0)),
            pl.BlockSpec((bt, bt), lambda i: (0, 0)),
        ],
        out_specs=pl.BlockSpec((bt, G_WIDTH), lambda i: (rev(i), 0)),
        out_shape=jax.ShapeDtypeStruct((t, G_WIDTH), BF16),
        scratch_shapes=[pltpu.VMEM((G_HEADS, G_V, G_QK), F32)],
        compiler_params=_cparams(("arbitrary",)),
        name="gla_bwd" if reverse else "gla_fwd",
    )(proj, proj, proj, small, wlr, blr, tri)


def _head_norm(hsum, w, nheads, hd):
    outs = []
    for h in range(nheads):
        x = hsum[:, h * hd:(h + 1) * hd]
        outs.append(x * lax.rsqrt(jnp.mean(x * x, axis=-1, keepdims=True) + EPS))
    return jnp.concatenate(outs, axis=-1) * w


def _outproj_kernel(xa_ref, xb_ref, mf_ref, mb_ref, gf_ref, gb_ref, mo_ref, gr_ref, mnw_ref, gnw_ref, wout_ref,
                    n2w_ref, rw_ref, rb_ref, x1_ref, xn_ref, idx_ref, gate_ref, *, n_experts, nb_a):
    hm = mf_ref[...].astype(F32) + mb_ref[...].astype(F32)
    a_out = _head_norm(hm, mnw_ref[...], M_HEADS, M_V) * jax.nn.sigmoid(mo_ref[...].astype(F32))
    hg = gf_ref[...].astype(F32) + gb_ref[...].astype(F32)
    r = gr_ref[...].astype(F32)
    b_out = _head_norm(hg, gnw_ref[...], G_HEADS, G_V) * (r * jax.nn.sigmoid(r))
    mixed = jnp.concatenate([a_out, b_out], axis=-1).astype(BF16)
    x = jnp.where(pl.program_id(0) < nb_a, xa_ref[...], xb_ref[...])
    x1 = x + _dot(mixed, wout_ref[...])
    x1_ref[...] = x1
    xn = x1 * lax.rsqrt(jnp.mean(x1 * x1, axis=-1, keepdims=True) + EPS) * n2w_ref[...]
    xn_ref[...] = xn
    x_hi = xn.astype(BF16)
    x_lo = (xn - x_hi.astype(F32)).astype(BF16)
    r_hi = _dot(x_hi, rw_ref[...])
    logits = (r_hi[:, :n_experts] + r_hi[:, n_experts:] + _dot(x_lo, rw_ref[:, :n_experts])
              + rb_ref[...])
    lane = lax.broadcasted_iota(jnp.int32, logits.shape, 1).astype(F32)
    vals, idxs = [], []
    cur = logits
    for _ in range(TOP_K):
        m = jnp.max(cur, axis=1, keepdims=True)
        sel = jnp.min(jnp.where(cur == m, lane, float(n_experts)), axis=1, keepdims=True)
        vals.append(m)
        idxs.append(sel)
        cur = jnp.where(lane == sel, NEG_INF, cur)
    exps = [jnp.exp(v - vals[0]) for v in vals]
    tot = exps[0] + exps[1] + exps[2] + exps[3]
    gate_ref[...] = jnp.concatenate(exps, axis=1) / tot
    idx_ref[...] = jnp.concatenate(idxs, axis=1).astype(jnp.int32)


def _split_hi_lo(w):
    hi = w.astype(BF16)
    return jnp.concatenate([hi, (w - hi.astype(F32)).astype(BF16)], axis=1)


def _out_proj(xa, xb, hm_f, hm_b, hg_f, hg_b, proj, mnw, gnw, w_out, n2w, router_w, router_b, bm):
    t = xa.shape[0] + xb.shape[0]
    nb_a = xa.shape[0] // bm
    n_e = router_w.shape[1] // 2
    row = lambda c: pl.BlockSpec((bm, 1024), lambda i: (i, c))
    full = lambda a: pl.BlockSpec(a.shape, lambda i: (0,) * a.ndim)
    args = (xa, xb, hm_f, hm_b, hg_f, hg_b, proj, proj, mnw, gnw, w_out, n2w, router_w, router_b)
    return pl.pallas_call(
        functools.partial(_outproj_kernel, n_experts=n_e, nb_a=nb_a),
        grid=(t // bm,),
        in_specs=_row_specs(bm, nb_a, D_MODEL) + [row(0), row(0), row(0), row(0), row(2), row(5)]
        + [full(a) for a in args[8:]],
        out_specs=[pl.BlockSpec((bm, D_MODEL), lambda i: (i, 0)),
                   pl.BlockSpec((bm, D_MODEL), lambda i: (i, 0)),
                   pl.BlockSpec((bm, TOP_K), lambda i: (i, 0)),
                   pl.BlockSpec((bm, TOP_K), lambda i: (i, 0))],
        out_shape=[jax.ShapeDtypeStruct((t, D_MODEL), F32),
                   jax.ShapeDtypeStruct((t, D_MODEL), F32),
                   jax.ShapeDtypeStruct((t, TOP_K), jnp.int32),
                   jax.ShapeDtypeStruct((t, TOP_K), F32)],
        compiler_params=_cparams(("arbitrary",)),
        name="out_proj_router",
    )(*args)


DEINT = 512


def _deinterleave_kernel(w_ref, perm_ref, out_ref, *, bf):
    half = DEINT // 2
    for c in range(2 * bf // DEINT):
        w16 = w_ref[0, :, c * DEINT:(c + 1) * DEINT].astype(BF16)
        res = _dot(w16, perm_ref[...]).astype(BF16)
        out_ref[0, 0, :, c * half:(c + 1) * half] = res[:, :half]
        out_ref[0, 0, :, bf + c * half:bf + (c + 1) * half] = res[:, half:]


def _deinterleave_gate_up(w_gate_up, bf):
    n_e, d, two_f = w_gate_up.shape
    nf = two_f // (2 * bf)
    r = jnp.arange(DEINT)[:, None]
    c = jnp.arange(DEINT)[None, :]
    perm = jnp.where(c < DEINT // 2, r == 2 * c, r == 2 * (c - DEINT // 2) + 1).astype(BF16)
    return pl.pallas_call(
        functools.partial(_deinterleave_kernel, bf=bf),
        grid=(n_e, nf),
        in_specs=[pl.BlockSpec((1, d, 2 * bf), lambda e, f: (e, 0, f)),
                  pl.BlockSpec((DEINT, DEINT), lambda e, f: (0, 0))],
        out_specs=pl.BlockSpec((1, 1, d, 2 * bf), lambda e, f: (e, f, 0, 0)),
        out_shape=jax.ShapeDtypeStruct((n_e, nf, d, 2 * bf), BF16),
        compiler_params=_cparams(("arbitrary", "arbitrary")),
        name="deinterleave_gate_up",
    )(w_gate_up, perm)


def _moe_kernel(be_ref, nu_ref, x_ref, wgu_ref, wd_ref, bgu_ref, bd_ref, out_ref, *, bf):
    i = pl.program_id(0)
    f = pl.program_id(1)
    used = i < nu_ref[0]

    @pl.when(jnp.logical_and(jnp.logical_not(used), f == 0))
    def _():
        out_ref[...] = jnp.zeros_like(out_ref)

    @pl.when(used)
    def _():
        gu = _dot(x_ref[...].astype(BF16), wgu_ref[0, 0]) + bgu_ref[0, 0]
        g = jnp.minimum(gu[:, :bf], SWIGLU_LIMIT)
        u = jnp.clip(gu[:, bf:], -SWIGLU_LIMIT, SWIGLU_LIMIT)
        act = ((u + 1.0) * g * jax.nn.sigmoid(g * SWIGLU_ALPHA)).astype(BF16)
        contrib = _dot(act, wd_ref[0])

        @pl.when(f == 0)
        def _():
            out_ref[...] = contrib + bd_ref[0]

        @pl.when(f != 0)
        def _():
            out_ref[...] += contrib


def _moe_experts(x_sorted, block_e, n_used, wgu, wd, bgu, bd, bm):
    rows = x_sorted.shape[0]
    nblocks = rows // bm
    nf, bf = wgu.shape[1], wgu.shape[3] // 2

    def fe(i, f, be, nu):
        return jnp.where(i < nu[0], f, nf - 1)

    grid_spec = pltpu.PrefetchScalarGridSpec(
        num_scalar_prefetch=2,
        grid=(nblocks, nf),
        in_specs=[
            pl.BlockSpec((bm, D_MODEL), lambda i, f, be, nu: (jnp.minimum(i, nu[0] - 1), 0)),
            pl.BlockSpec((1, 1, D_MODEL, 2 * bf), lambda i, f, be, nu: (be[i], fe(i, f, be, nu), 0, 0)),
            pl.BlockSpec((1, bf, D_MODEL), lambda i, f, be, nu: (be[i], fe(i, f, be, nu), 0)),
            pl.BlockSpec((1, 1, 1, 2 * bf), lambda i, f, be, nu: (be[i], fe(i, f, be, nu), 0, 0)),
            pl.BlockSpec((1, 1, D_MODEL), lambda i, f, be, nu: (be[i], 0, 0)),
        ],
        out_specs=pl.BlockSpec((bm, D_MODEL), lambda i, f, be, nu: (i, 0)),
    )
    return pl.pallas_call(
        functools.partial(_moe_kernel, bf=bf),
        grid_spec=grid_spec,
        out_shape=jax.ShapeDtypeStruct((rows, D_MODEL), F32),
        compiler_params=_cparams(("arbitrary", "arbitrary")),
        name="moe_experts",
    )(block_e, n_used, x_sorted, wgu, wd, bgu, bd)


def _row_copy(src_ref, src_row, dst_ref, dst_row, sem):
    return pltpu.make_async_copy(src_ref.at[pl.ds(src_row, 1), :], dst_ref.at[pl.ds(dst_row, 1), :], sem)


def _dispatch_kernel(pe_ref, dest_ref, xn_ref, xs_ref, zero_ref, sem, zsem, *, bm, bm_moe, n_experts):
    @pl.when(pl.program_id(0) == 0)
    def _():
        zero_ref[...] = jnp.zeros_like(zero_ref)

        def last_block(e):
            start = pl.multiple_of(pe_ref[e + 1] - bm_moe, bm_moe)
            return pltpu.make_async_copy(zero_ref, xs_ref.at[pl.ds(start, bm_moe), :], zsem)

        for e in range(n_experts):
            @pl.when(pe_ref[e + 1] > pe_ref[e])
            def _():
                last_block(e).start()
        for e in range(n_experts):
            @pl.when(pe_ref[e + 1] > pe_ref[e])
            def _():
                last_block(e).wait()

        def unused_block(b, carry):
            cp = pltpu.make_async_copy(zero_ref, xs_ref.at[pl.ds(pl.multiple_of(b * bm_moe, bm_moe), bm_moe), :], zsem)
            cp.start()
            cp.wait()
            return carry
        lax.fori_loop(pe_ref[n_experts] // bm_moe, xs_ref.shape[0] // bm_moe, unused_block, 0)

    def issue(r, carry):
        for k in range(TOP_K):
            _row_copy(xn_ref, r, xs_ref, dest_ref[TOP_K * r + k], sem).start(priority=k % 2)
        return carry

    def drain(r, carry):
        for k in range(TOP_K):
            _row_copy(xn_ref, r, xs_ref, dest_ref[TOP_K * r + k], sem).wait()
        return carry

    lax.fori_loop(0, bm, issue, 0, unroll=8)
    lax.fori_loop(0, bm, drain, 0, unroll=8)


def _dispatch(xn, dest, pad_edges, rows, bm, bm_moe):
    t = xn.shape[0]
    n_e = pad_edges.shape[0] - 1
    grid_spec = pltpu.PrefetchScalarGridSpec(
        num_scalar_prefetch=1,
        grid=(t // bm,),
        in_specs=[pl.BlockSpec((bm * TOP_K,), lambda i, pe: (i,), memory_space=pltpu.SMEM),
                  pl.BlockSpec((bm, D_MODEL), lambda i, pe: (i, 0))],
        out_specs=pl.BlockSpec(memory_space=pl.ANY),
        scratch_shapes=[pltpu.VMEM((bm_moe, D_MODEL), F32),
                        pltpu.SemaphoreType.DMA(()), pltpu.SemaphoreType.DMA(())],
    )
    return pl.pallas_call(
        functools.partial(_dispatch_kernel, bm=bm, bm_moe=bm_moe, n_experts=n_e),
        grid_spec=grid_spec,
        out_shape=jax.ShapeDtypeStruct((rows, D_MODEL), F32),
        compiler_params=_cparams(("arbitrary",)),
        name="dispatch_rows",
    )(pad_edges, dest, xn)


def _combine_kernel(dcur_ref, dnxt_ref, x1_ref, gate_ref, fw_ref, ys_ref, outa_ref, outb_ref, ybuf, sems,
                    *, bm, nb_a):
    i = pl.program_id(0)
    nb = pl.num_programs(0)
    slot = lax.rem(i, 2)

    def issue(dref, s):
        def body(r, carry):
            for k in range(TOP_K):
                _row_copy(ys_ref, dref[TOP_K * r + k], ybuf.at[s, k], r, sems.at[s]).start(priority=k % 2)
            return carry
        lax.fori_loop(0, bm, body, 0, unroll=8)

    @pl.when(i == 0)
    def _():
        issue(dcur_ref, 0)

    @pl.when(i + 1 < nb)
    def _():
        issue(dnxt_ref, 1 - slot)

    def drain(r, carry):
        for k in range(TOP_K):
            _row_copy(ys_ref, dcur_ref[TOP_K * r + k], ybuf.at[slot, k], r, sems.at[slot]).wait()
        return carry
    lax.fori_loop(0, bm, drain, 0, unroll=8)

    gate = gate_ref[...]
    y = (ybuf[slot, 0] * gate[:, 0:1] + ybuf[slot, 1] * gate[:, 1:2]
         + ybuf[slot, 2] * gate[:, 2:3] + ybuf[slot, 3] * gate[:, 3:4])
    x2 = x1_ref[...] + y
    y_out = x2 * lax.rsqrt(jnp.mean(x2 * x2, axis=-1, keepdims=True) + EPS) * fw_ref[...]

    @pl.when(i < nb_a)
    def _():
        outa_ref[...] = y_out

    @pl.when(i >= nb_a)
    def _():
        outb_ref[...] = y_out


def _combine(x1, ys, dest, gates, final_w, bm, rows_a):
    t = x1.shape[0]
    nb = t // bm
    nb_a = rows_a // bm
    blk = pl.BlockSpec((bm, D_MODEL), lambda i: (i, 0))
    return pl.pallas_call(
        functools.partial(_combine_kernel, bm=bm, nb_a=nb_a),
        grid=(nb,),
        in_specs=[pl.BlockSpec((bm * TOP_K,), lambda i: (i,), memory_space=pltpu.SMEM),
                  pl.BlockSpec((bm * TOP_K,), lambda i: (jnp.minimum(i + 1, nb - 1),), memory_space=pltpu.SMEM),
                  blk,
                  pl.BlockSpec((bm, TOP_K), lambda i: (i, 0)),
                  pl.BlockSpec((1, D_MODEL), lambda i: (0, 0)),
                  pl.BlockSpec(memory_space=pl.ANY)],
        out_specs=_row_specs(bm, nb_a, D_MODEL),
        out_shape=[jax.ShapeDtypeStruct((rows_a, D_MODEL), F32),
                   jax.ShapeDtypeStruct((t - rows_a, D_MODEL), F32)],
        scratch_shapes=[pltpu.VMEM((2, TOP_K, bm, D_MODEL), F32), pltpu.SemaphoreType.DMA((2,))],
        compiler_params=_cparams(("arbitrary",)),
        name="combine_final_norm",
    )(dest, dest, x1, gates, final_w, ys)


def _route(top_idx, n_experts, bm):
    t = top_idx.shape[0]
    n_assign = t * TOP_K
    flat_e = top_idx.reshape(n_assign)
    onehot = (flat_e[:, None] == jnp.arange(n_experts, dtype=jnp.int32)[None, :]).astype(jnp.int32)
    csum = jnp.cumsum(onehot, axis=0)
    rank = jnp.sum((csum - onehot) * onehot, axis=1)
    counts = csum[-1]
    padded = ((counts + bm - 1) // bm) * bm
    pad_end = jnp.cumsum(padded)
    pad_start = pad_end - padded
    dest = (pad_start[flat_e] + rank).astype(jnp.int32)
    nblocks = n_assign // bm + n_experts
    n_used = (pad_end[-1] // bm).astype(jnp.int32).reshape(1)
    block_start = jnp.arange(nblocks, dtype=jnp.int32) * bm
    block_e = jnp.minimum(jnp.searchsorted(pad_end, block_start, side="right"), n_experts - 1)
    last_e = block_e[jnp.maximum(n_used[0] - 1, 0)]
    block_e = jnp.where(jnp.arange(nblocks) < n_used[0], block_e, last_e).astype(jnp.int32)
    pad_edges = jnp.concatenate([jnp.zeros((1,), jnp.int32), pad_end.astype(jnp.int32)])
    return dest, pad_edges, block_e, n_used, nblocks * bm


def _trunk(xa, xb, seq_lens, norm1_w, w_in, mlstm_gate_b, mlstm_norm_w, gla_w_lr, gla_b_lr, gla_norm_w, w_out,
           norm2_w, router_w, router_b, w_gate_up, b_gate_up, w_down, b_down, final_norm_w,
           *, bm_in=512, bn_in=2048, bt_mix=256, bm_out=256, bm_disp=256, bm_moe=512, bf_moe=512, bm_comb=256):
    n_e = router_w.shape[1]
    w_main = jnp.concatenate([w_in[:, :3072], w_in[:, 3088:6160]], axis=1).astype(BF16)
    w_small = jnp.zeros((D_MODEL, SMALL_WIDTH), F32)
    w_small = w_small.at[:, 0:16].set(w_in[:, 3072:3088]).at[:, 16:48].set(w_in[:, 6160:6192]).astype(BF16)
    w_kt = w_in[:, 512:1024].T.astype(BF16)
    proj, small, small_t, k_t = _in_proj(xa, xb, norm1_w.reshape(1, D_MODEL), w_main, w_small, w_small.T, w_kt,
                                         bm_in, bn_in)

    hm_f = _mlstm_dir(proj, k_t, small, small_t, mlstm_gate_b, seq_lens, bt_mix, False)
    hm_b = _mlstm_dir(proj, k_t, small, small_t, mlstm_gate_b, seq_lens, bt_mix, True)
    hg_f = _gla_dir(proj, small, gla_w_lr, gla_b_lr, seq_lens, bt_mix, False)
    hg_b = _gla_dir(proj, small, gla_w_lr, gla_b_lr, seq_lens, bt_mix, True)

    x1, xn, top_idx, gates = _out_proj(
        xa, xb, hm_f, hm_b, hg_f, hg_b, proj, mlstm_norm_w.reshape(1, M_WIDTH), gla_norm_w.reshape(1, G_WIDTH),
        w_out.astype(BF16), norm2_w.reshape(1, D_MODEL), _split_hi_lo(router_w.astype(F32)),
        router_b.reshape(1, n_e).astype(F32), bm_out)

    dest, pad_edges, block_e, n_used, rows = _route(top_idx, n_e, bm_moe)
    x_sorted = _dispatch(xn, dest, pad_edges, rows, bm_disp, bm_moe)
    nf = w_gate_up.shape[2] // (2 * bf_moe)
    wgu = _deinterleave_gate_up(w_gate_up, bf_moe)
    bgu = jnp.concatenate([b_gate_up[:, 0::2].reshape(n_e, nf, 1, bf_moe),
                           b_gate_up[:, 1::2].reshape(n_e, nf, 1, bf_moe)], axis=-1).astype(F32)
    ys = _moe_experts(x_sorted, block_e, n_used, wgu, w_down.astype(BF16), bgu,
                      b_down[:, None, :].astype(F32), bm_moe)
    return _combine(x1, ys, dest, gates, final_norm_w.reshape(1, D_MODEL), bm_comb, xa.shape[0])


def kernel(x_prompt, x_sample, norm1_w, w_in, mlstm_gate_b, mlstm_norm_w, gla_w_lr, gla_b_lr, gla_norm_w,
           w_out, norm2_w, router_w, router_b, w_gate_up, b_gate_up, w_down, b_down, final_norm_w):
    bp, sp, _ = x_prompt.shape
    bs, ss, _ = x_sample.shape
    seq_lens = (sp,) * bp + (ss,) * bs
    ya, yb = _trunk(x_prompt.reshape(bp * sp, D_MODEL), x_sample.reshape(bs * ss, D_MODEL), seq_lens,
                    norm1_w[0], w_in[0], mlstm_gate_b[0], mlstm_norm_w[0], gla_w_lr[0], gla_b_lr[0],
                    gla_norm_w[0], w_out[0], norm2_w[0], router_w[0], router_b[0], w_gate_up[0], b_gate_up[0],
                    w_down[0], b_down[0], final_norm_w)
    return (ya.reshape(bp, sp, D_MODEL), yb.reshape(bs, ss, D_MODEL))
```

```python
import functools

import jax
import jax.numpy as jnp
from jax import lax
from jax.experimental import pallas as pl
from jax.experimental.pallas import tpu as pltpu

F32 = jnp.float32
BF16 = jnp.bfloat16

D_MODEL = 2048
CHUNK = 64
M_HEADS = 4
M_QK = 128
M_V = 256
M_WIDTH = M_HEADS * M_V
GATE_CAP = 15.0
G_HEADS = 4
G_QK = 128
G_V = 256
G_WIDTH = G_HEADS * G_V
G_RANK = 16
G_TAU = 16.0
TOP_K = 4
SWIGLU_LIMIT = 7.0
SWIGLU_ALPHA = 1.702
EPS = 1e-6
MAIN_WIDTH = 2 * M_HEADS * M_QK + 2 * M_WIDTH + 2 * G_HEADS * G_QK + 2 * G_WIDTH
SMALL_WIDTH = 128
NEG_INF = float("-inf")

VMEM_LIMIT = 56 * 1024 * 1024


def _cparams(sem):
    return pltpu.CompilerParams(dimension_semantics=sem, vmem_limit_bytes=VMEM_LIMIT)


def _log_sigmoid(x):
    return jnp.minimum(x, 0.0) - jnp.log1p(jnp.exp(-jnp.abs(x)))


def _split3(x):
    hi = x.astype(BF16)
    r = x - hi.astype(F32)
    mid = r.astype(BF16)
    lo = (r - mid.astype(F32)).astype(BF16)
    return hi, mid, lo


def _dot(a, b):
    return jnp.dot(a, b, preferred_element_type=F32)


def _dot_nt(a, b):
    return lax.dot_general(a, b, (((1,), (1,)), ((), ())), preferred_element_type=F32)


def _dot_tn(a, b):
    return lax.dot_general(a, b, (((0,), (0,)), ((), ())), preferred_element_type=F32)


def _cum_left(tri, x):
    hi, mid, lo = _split3(x)
    return _dot(tri, hi) + _dot(tri, mid) + _dot(tri, lo)


def _cum_right(x, tri):
    hi, mid, lo = _split3(x)
    return _dot(hi, tri) + _dot(mid, tri) + _dot(lo, tri)


def _inproj_kernel(xa_ref, xb_ref, nw_ref, w_ref, ws_ref, wst_ref, wkt_ref, main_ref, small_ref, smallt_ref, kt_ref,
                   hn_ref, *, nb_a):
    @pl.when(pl.program_id(1) == 0)
    def _():
        x = jnp.where(pl.program_id(0) < nb_a, xa_ref[...], xb_ref[...])
        y = x * lax.rsqrt(jnp.mean(x * x, axis=-1, keepdims=True) + EPS) * nw_ref[...]
        hb = y.astype(BF16)
        hn_ref[...] = hb
        small_ref[...] = _dot(hb, ws_ref[...])
        smallt_ref[...] = _dot_nt(wst_ref[...], hb)
        kt_ref[...] = _dot_nt(wkt_ref[...], hb).astype(BF16)

    main_ref[...] = _dot(hn_ref[...], w_ref[...]).astype(BF16)


def _row_specs(bm, nb_a, width):
    return [pl.BlockSpec((bm, width), lambda i, *_: (jnp.minimum(i, nb_a - 1), 0)),
            pl.BlockSpec((bm, width), lambda i, *_: (jnp.maximum(i - nb_a, 0), 0))]


def _in_proj(xa, xb, norm_w, w_main, w_small, w_small_t, w_kt, bm, bn):
    t = xa.shape[0] + xb.shape[0]
    nb_a = xa.shape[0] // bm
    return pl.pallas_call(
        functools.partial(_inproj_kernel, nb_a=nb_a),
        grid=(t // bm, MAIN_WIDTH // bn),
        in_specs=_row_specs(bm, nb_a, D_MODEL) + [
            pl.BlockSpec((1, D_MODEL), lambda i, j: (0, 0)),
            pl.BlockSpec((D_MODEL, bn), lambda i, j: (0, j)),
            pl.BlockSpec((D_MODEL, SMALL_WIDTH), lambda i, j: (0, 0)),
            pl.BlockSpec((SMALL_WIDTH, D_MODEL), lambda i, j: (0, 0)),
            pl.BlockSpec((M_HEADS * M_QK, D_MODEL), lambda i, j: (0, 0)),
        ],
        out_specs=[
            pl.BlockSpec((bm, bn), lambda i, j: (i, j)),
            pl.BlockSpec((bm, SMALL_WIDTH), lambda i, j: (i, 0)),
            pl.BlockSpec((SMALL_WIDTH, bm), lambda i, j: (0, i)),
            pl.BlockSpec((M_HEADS * M_QK, bm), lambda i, j: (0, i)),
        ],
        out_shape=[
            jax.ShapeDtypeStruct((t, MAIN_WIDTH), BF16),
            jax.ShapeDtypeStruct((t, SMALL_WIDTH), F32),
            jax.ShapeDtypeStruct((SMALL_WIDTH, t), F32),
            jax.ShapeDtypeStruct((M_HEADS * M_QK, t), BF16),
        ],
        scratch_shapes=[pltpu.VMEM((bm, D_MODEL), BF16)],
        compiler_params=_cparams(("arbitrary", "arbitrary")),
        name="in_proj",
    )(xa, xb, norm_w, w_main, w_small, w_small_t, w_kt)


def _chunk_masks(reverse):
    row = lax.broadcasted_iota(jnp.int32, (CHUNK, CHUNK), 0)
    col = lax.broadcasted_iota(jnp.int32, (CHUNK, CHUNK), 1)
    return (col >= row) if reverse else (col <= row)


def _reset_pred(blk, reset_blocks):
    pred = blk == reset_blocks[0]
    for rb in reset_blocks[1:]:
        pred = jnp.logical_or(pred, blk == rb)
    return pred


def _mlstm_kernel(q_ref, k_ref, kt_ref, v_ref, sm_ref, smt_ref, gb_ref, gbt_ref, tri_ref, trit_ref,
                  h_ref, c_ref, n_ref, m_ref, *, reverse, nblocks, reset_blocks, bt):
    i = pl.program_id(0)
    blk = (nblocks - 1 - i) if reverse else i

    @pl.when(_reset_pred(blk, reset_blocks))
    def _():
        c_ref[...] = jnp.zeros_like(c_ref)
        n_ref[...] = jnp.zeros_like(n_ref)
        m_ref[...] = jnp.zeros_like(m_ref)

    d = 1 if reverse else 0
    scale = M_QK ** -0.5
    gates = GATE_CAP * jnp.tanh((sm_ref[...] + gb_ref[...]) / GATE_CAP)
    bcum = _cum_left(tri_ref[...], _log_sigmoid(gates))
    gates_t = GATE_CAP * jnp.tanh((smt_ref[0:16, :] + gbt_ref[...]) / GATE_CAP)
    bcum_t = _cum_right(_log_sigmoid(gates_t), trit_ref[...])
    i_row = gates_t[8 * d:8 * d + 4, :]
    b_row = bcum_t[8 * d + 4:8 * d + 8, :]
    last = 0 if reverse else bt - 1
    g4 = b_row[:, last:last + 1]
    w4 = g4 - b_row + i_row
    mloc4 = jnp.max(w4, axis=1, keepdims=True)
    e4 = jnp.exp(w4 - mloc4)
    d4 = i_row - b_row
    row = lax.broadcasted_iota(jnp.int32, (bt, bt), 0)
    col = lax.broadcasted_iota(jnp.int32, (bt, bt), 1)
    mask = (col >= row) if reverse else (col <= row)
    ones_b = jnp.ones((bt, 128), BF16)
    e4b = e4.astype(BF16)
    outs = []
    for h in range(M_HEADS):
        fl = 8 * d + 4 + h
        g, m_loc = g4[h:h + 1], mloc4[h:h + 1]
        qb = q_ref[:, h * M_QK:(h + 1) * M_QK]
        kb = k_ref[:, h * M_QK:(h + 1) * M_QK]
        ktb = kt_ref[h * M_QK:(h + 1) * M_QK, :]
        vb = v_ref[:, h * M_V:(h + 1) * M_V]
        ekt = (ktb.astype(F32) * e4[h:h + 1]).astype(BF16)
        kv_loc = _dot(ekt, vb)
        n_loc = _dot(e4b, kb)[h:h + 1]
        s = _dot(qb, ktb) * scale
        b_b = jnp.broadcast_to(bcum[:, fl:fl + 1], (bt, 128))
        a_intra = jnp.where(mask, jnp.tile(b_b, (1, bt // 128)) + d4[h:h + 1], NEG_INF)
        m_intra = jnp.max(a_intra, axis=1, keepdims=True)
        pb = (jnp.exp(a_intra - m_intra) * s).astype(BF16)
        pv = _dot(pb, vb)
        p_sum = _dot(pb, ones_b)

        m_in, c_in, n_in = m_ref[h], c_ref[h], n_ref[h]
        m_intra_b = jnp.broadcast_to(m_intra, (bt, 128))
        a_inter = b_b + m_in
        m_row = jnp.maximum(m_intra_b, a_inter)
        f_intra = jnp.exp(m_intra_b - m_row)
        f_inter = jnp.exp(a_inter - m_row)
        qn = _dot_nt(qb, jnp.broadcast_to(n_in, (128, M_QK)).astype(BF16)) * scale
        den = f_intra * p_sum + f_inter * qn
        rinv = 1.0 / jnp.maximum(jnp.abs(den), jnp.exp(-m_row))
        qc = _dot(qb, c_in.astype(BF16)) * scale
        w_intra = jnp.tile(f_intra * rinv, (1, M_V // 128))
        w_inter = jnp.tile(f_inter * rinv, (1, M_V // 128))
        outs.append((w_intra * pv + w_inter * qc).astype(h_ref.dtype))

        m_new = jnp.maximum(g + m_in, m_loc)
        aa = jnp.exp(g + m_in - m_new)
        bb = jnp.exp(m_loc - m_new)
        c_ref[h] = aa * c_in + bb * kv_loc
        n_ref[h] = aa * n_in + bb * n_loc
        m_ref[h] = m_new
    h_ref[...] = jnp.concatenate(outs, axis=1)


def _gla_gate_kernel(sm_ref, wlr_ref, blr_ref, trif_ref, trib_ref, bf_ref, bb_ref):
    z = jnp.dot(sm_ref[...], wlr_ref[...], preferred_element_type=F32,
                precision=lax.Precision.HIGHEST) + blr_ref[...]
    log_a = _log_sigmoid(z) * (1.0 / G_TAU)
    w = G_HEADS * G_QK
    bf_ref[...] = _cum_left(trif_ref[...], log_a[:, :w])
    bb_ref[...] = _cum_left(trib_ref[...], log_a[:, w:])


def _gla_gates(small, w_lr, b_lr, bt):
    t = small.shape[0]
    w = G_HEADS * G_QK
    wlr = jnp.zeros((SMALL_WIDTH, 2 * w), F32)
    wlr = wlr.at[16:32, :w].set(w_lr[0].astype(F32)).at[32:48, w:].set(w_lr[1].astype(F32))
    blr = jnp.concatenate([b_lr[0], b_lr[1]]).astype(F32).reshape(1, 2 * w)
    out = pl.BlockSpec((bt, w), lambda i: (i, 0))
    return pl.pallas_call(
        _gla_gate_kernel,
        grid=(t // bt,),
        in_specs=[pl.BlockSpec((bt, SMALL_WIDTH), lambda i: (i, 0)),
                  pl.BlockSpec((SMALL_WIDTH, 2 * w), lambda i: (0, 0)),
                  pl.BlockSpec((1, 2 * w), lambda i: (0, 0)),
                  pl.BlockSpec((bt, bt), lambda i: (0, 0)),
                  pl.BlockSpec((bt, bt), lambda i: (0, 0))],
        out_specs=[out, out],
        out_shape=[jax.ShapeDtypeStruct((t, w), F32), jax.ShapeDtypeStruct((t, w), F32)],
        compiler_params=_cparams(("arbitrary",)),
        name="gla_gates",
    )(small, wlr, blr, _block_tri(bt, False), _block_tri(bt, True))


def _gla_kernel(q_ref, k_ref, v_ref, b_ref, o_ref, s_ref, *, reverse, nblocks, reset_blocks, bt):
    i = pl.program_id(0)
    blk = (nblocks - 1 - i) if reverse else i

    @pl.when(_reset_pred(blk, reset_blocks))
    def _():
        s_ref[...] = jnp.zeros_like(s_ref)

    scale = G_QK ** -0.5
    bcum = b_ref[...]
    mask = _chunk_masks(reverse)
    nchunks = bt // CHUNK
    order = range(nchunks - 1, -1, -1) if reverse else range(nchunks)
    sts = [s_ref[h] for h in range(G_HEADS)]
    for c in order:
        r0 = c * CHUNK
        outs = []
        for h in range(G_HEADS):
            b = bcum[r0:r0 + CHUNK, h * G_QK:(h + 1) * G_QK]
            g = b[0:1] if reverse else b[CHUNK - 1:CHUNK]
            qf = q_ref[r0:r0 + CHUNK, h * G_QK:(h + 1) * G_QK].astype(F32)
            kf = k_ref[r0:r0 + CHUNK, h * G_QK:(h + 1) * G_QK].astype(F32)
            vb = v_ref[r0:r0 + CHUNK, h * G_V:(h + 1) * G_V]
            q_dec = (qf * scale * jnp.exp(b)).astype(BF16)
            k_inv = (kf * jnp.exp(-b)).astype(BF16)
            k_end = (kf * jnp.exp(g - b)).astype(BF16)
            att = jnp.where(mask, _dot_nt(q_dec, k_inv), 0.0)
            o = _dot(att.astype(BF16), vb) + _dot_nt(q_dec, sts[h].astype(BF16))
            outs.append(o.astype(o_ref.dtype))
            sts[h] = sts[h] * jnp.exp(g) + _dot_tn(vb, k_end)
        o_ref[r0:r0 + CHUNK, :] = jnp.concatenate(outs, axis=1)
    for h in range(G_HEADS):
        s_ref[h] = sts[h]


def _block_tri(bt, upper, chunk=CHUNK):
    r = jnp.arange(bt)
    same = (r[:, None] // chunk) == (r[None, :] // chunk)
    tri = (r[None, :] >= r[:, None]) if upper else (r[None, :] <= r[:, None])
    return (same & tri).astype(BF16)


def _reset_blocks(seq_lens, bt, reverse):
    out, off = [], 0
    for s in seq_lens:
        assert s % bt == 0
        out.append((off + s) // bt - 1 if reverse else off // bt)
        off += s
    return tuple(out)


def _mlstm_dir(proj, k_t, small, small_t, gate_b, seq_lens, bt, reverse):
    t = proj.shape[0]
    nb = t // bt
    tri = _block_tri(bt, reverse, chunk=bt)
    rev = (lambda i: nb - 1 - i) if reverse else (lambda i: i)
    gb = jnp.zeros((1, SMALL_WIDTH), F32).at[0, :16].set(gate_b.reshape(16))
    gbt = gate_b.reshape(16, 1)
    kern = functools.partial(_mlstm_kernel, reverse=reverse, nblocks=nb,
                             reset_blocks=_reset_blocks(seq_lens, bt, reverse), bt=bt)
    return pl.pallas_call(
        kern,
        grid=(nb,),
        in_specs=[
            pl.BlockSpec((bt, 512), lambda i: (rev(i), 0)),
            pl.BlockSpec((bt, 512), lambda i: (rev(i), 1)),
            pl.BlockSpec((M_HEADS * M_QK, bt), lambda i: (0, rev(i))),
            pl.BlockSpec((bt, 1024), lambda i: (rev(i), 1)),
            pl.BlockSpec((bt, SMALL_WIDTH), lambda i: (rev(i), 0)),
            pl.BlockSpec((SMALL_WIDTH, bt), lambda i: (0, rev(i))),
            pl.BlockSpec((1, SMALL_WIDTH), lambda i: (0, 0)),
            pl.BlockSpec((16, 1), lambda i: (0, 0)),
            pl.BlockSpec((bt, bt), lambda i: (0, 0)),
            pl.BlockSpec((bt, bt), lambda i: (0, 0)),
        ],
        out_specs=pl.BlockSpec((bt, M_WIDTH), lambda i: (rev(i), 0)),
        out_shape=jax.ShapeDtypeStruct((t, M_WIDTH), BF16),
        scratch_shapes=[pltpu.VMEM((M_HEADS, M_QK, M_V), F32),
                        pltpu.VMEM((M_HEADS, 1, M_QK), F32),
                        pltpu.VMEM((M_HEADS, 1, 1), F32)],
        compiler_params=_cparams(("arbitrary",)),
        name="mlstm_bwd" if reverse else "mlstm_fwd",
    )(proj, proj, k_t, proj, small, small_t, gb, gbt, tri, tri.T)


def _gla_dir(proj, bcum, seq_lens, bt, reverse):
    t = proj.shape[0]
    nb = t // bt
    rev = (lambda i: nb - 1 - i) if reverse else (lambda i: i)
    kern = functools.partial(_gla_kernel, reverse=reverse, nblocks=nb,
                             reset_blocks=_reset_blocks(seq_lens, bt, reverse), bt=bt)
    return pl.pallas_call(
        kern,
        grid=(nb,),
        in_specs=[
            pl.BlockSpec((bt, 512), lambda i: (rev(i), 6)),
            pl.BlockSpec((bt, 512), lambda i: (rev(i), 7)),
            pl.BlockSpec((bt, 1024), lambda i: (rev(i), 4)),
            pl.BlockSpec((bt, G_HEADS * G_QK), lambda i: (rev(i), 0)),
        ],
        out_specs=pl.BlockSpec((bt, G_WIDTH), lambda i: (rev(i), 0)),
        out_shape=jax.ShapeDtypeStruct((t, G_WIDTH), BF16),
        scratch_shapes=[pltpu.VMEM((G_HEADS, G_V, G_QK), F32)],
        compiler_params=_cparams(("arbitrary",)),
        name="gla_bwd" if reverse else "gla_fwd",
    )(proj, proj, proj, bcum)


def _head_norm(hsum, w, nheads, hd):
    outs = []
    for h in range(nheads):
        x = hsum[:, h * hd:(h + 1) * hd]
        outs.append(x * lax.rsqrt(jnp.mean(x * x, axis=-1, keepdims=True) + EPS))
    return jnp.concatenate(outs, axis=-1) * w


def _outproj_kernel(xa_ref, xb_ref, mf_ref, mb_ref, gf_ref, gb_ref, mo_ref, gr_ref, mnw_ref, gnw_ref, wout_ref,
                    n2w_ref, rw_ref, rb_ref, lt_ref, x1_ref, xn_ref, idx_ref, gate_ref, rank_ref, cnt_ref,
                    *, n_experts, nb_a):
    hm = mf_ref[...].astype(F32) + mb_ref[...].astype(F32)
    a_out = _head_norm(hm, mnw_ref[...], M_HEADS, M_V) * jax.nn.sigmoid(mo_ref[...].astype(F32))
    hg = gf_ref[...].astype(F32) + gb_ref[...].astype(F32)
    r = gr_ref[...].astype(F32)
    b_out = _head_norm(hg, gnw_ref[...], G_HEADS, G_V) * (r * jax.nn.sigmoid(r))
    mixed = jnp.concatenate([a_out, b_out], axis=-1).astype(BF16)
    x = jnp.where(pl.program_id(0) < nb_a, xa_ref[...], xb_ref[...])
    x1 = x + _dot(mixed, wout_ref[...])
    x1_ref[...] = x1
    xn = x1 * lax.rsqrt(jnp.mean(x1 * x1, axis=-1, keepdims=True) + EPS) * n2w_ref[...]
    xn_ref[...] = xn
    x_hi = xn.astype(BF16)
    x_lo = (xn - x_hi.astype(F32)).astype(BF16)
    r_hi = _dot(x_hi, rw_ref[...])
    logits = (r_hi[:, :n_experts] + r_hi[:, n_experts:] + _dot(x_lo, rw_ref[:, :n_experts])
              + rb_ref[...])
    lane = lax.broadcasted_iota(jnp.int32, logits.shape, 1).astype(F32)
    vals, idxs = [], []
    cur = logits
    for _ in range(TOP_K):
        m = jnp.max(cur, axis=1, keepdims=True)
        sel = jnp.min(jnp.where(cur == m, lane, float(n_experts)), axis=1, keepdims=True)
        vals.append(m)
        idxs.append(sel)
        cur = jnp.where(lane == sel, NEG_INF, cur)
    exps = [jnp.exp(v - vals[0]) for v in vals]
    tot = exps[0] + exps[1] + exps[2] + exps[3]
    gate_ref[...] = jnp.concatenate(exps, axis=1) / tot
    idx_ref[...] = jnp.concatenate(idxs, axis=1).astype(jnp.int32)

    @pl.when(pl.program_id(0) == 0)
    def _():
        cnt_ref[...] = jnp.zeros_like(cnt_ref)

    hits = [lane == sel for sel in idxs]
    tok_oh = jnp.where(hits[0] | hits[1] | hits[2] | hits[3], 1.0, 0.0)
    before = _dot(lt_ref[...], tok_oh.astype(BF16)) + cnt_ref[...]
    rank_ref[...] = jnp.concatenate(
        [jnp.sum(jnp.where(hit, before, 0.0), axis=1, keepdims=True) for hit in hits], axis=1).astype(jnp.int32)
    cnt_ref[...] += jnp.sum(tok_oh, axis=0, keepdims=True)


def _split_hi_lo(w):
    hi = w.astype(BF16)
    return jnp.concatenate([hi, (w - hi.astype(F32)).astype(BF16)], axis=1)


def _out_proj(xa, xb, hm_f, hm_b, hg_f, hg_b, proj, mnw, gnw, w_out, n2w, router_w, router_b, bm):
    t = xa.shape[0] + xb.shape[0]
    nb_a = xa.shape[0] // bm
    n_e = router_w.shape[1] // 2
    row = lambda c: pl.BlockSpec((bm, 1024), lambda i: (i, c))
    full = lambda a: pl.BlockSpec(a.shape, lambda i: (0,) * a.ndim)
    r = jnp.arange(bm)
    lower = (r[None, :] < r[:, None]).astype(BF16)
    args = (xa, xb, hm_f, hm_b, hg_f, hg_b, proj, proj, mnw, gnw, w_out, n2w, router_w, router_b, lower)
    return pl.pallas_call(
        functools.partial(_outproj_kernel, n_experts=n_e, nb_a=nb_a),
        grid=(t // bm,),
        in_specs=_row_specs(bm, nb_a, D_MODEL) + [row(0), row(0), row(0), row(0), row(2), row(5)]
        + [full(a) for a in args[8:]],
        out_specs=[pl.BlockSpec((bm, D_MODEL), lambda i: (i, 0)),
                   pl.BlockSpec((bm, D_MODEL), lambda i: (i, 0)),
                   pl.BlockSpec((bm, TOP_K), lambda i: (i, 0)),
                   pl.BlockSpec((bm, TOP_K), lambda i: (i, 0)),
                   pl.BlockSpec((bm, TOP_K), lambda i: (i, 0)),
                   pl.BlockSpec((1, n_e), lambda i: (0, 0))],
        out_shape=[jax.ShapeDtypeStruct((t, D_MODEL), F32),
                   jax.ShapeDtypeStruct((t, D_MODEL), F32),
                   jax.ShapeDtypeStruct((t, TOP_K), jnp.int32),
                   jax.ShapeDtypeStruct((t, TOP_K), F32),
                   jax.ShapeDtypeStruct((t, TOP_K), jnp.int32),
                   jax.ShapeDtypeStruct((1, n_e), F32)],
        compiler_params=_cparams(("arbitrary",)),
        name="out_proj_router",
    )(*args)


DEINT = 512
WD_ROWS = 256


def _moe_kernel(be_ref, nu_ref, x_ref, wgu_ref, wd_ref, bgu_ref, bd_ref, out_ref, *, bf):
    i = pl.program_id(0)
    f = pl.program_id(1)
    used = i < nu_ref[0]

    @pl.when(jnp.logical_and(jnp.logical_not(used), f == 0))
    def _():
        out_ref[...] = jnp.zeros_like(out_ref)

    @pl.when(used)
    def _():
        xb = x_ref[...].astype(BF16)
        g = jnp.minimum(_dot(xb, wgu_ref[0, 0, 0]) + bgu_ref[0, 0, :, :bf], SWIGLU_LIMIT)
        u = jnp.clip(_dot(xb, wgu_ref[0, 0, 1]) + bgu_ref[0, 0, :, bf:], -SWIGLU_LIMIT, SWIGLU_LIMIT)
        act = ((u + 1.0) * g * jax.nn.sigmoid(g * SWIGLU_ALPHA)).astype(BF16)
        contrib = _dot(act, wd_ref[0])

        @pl.when(f == 0)
        def _():
            out_ref[...] = contrib + bd_ref[0]

        @pl.when(f != 0)
        def _():
            out_ref[...] += contrib


def _moe_experts(x_sorted, block_e, n_used, wgu, wd, bgu, bd, bm):
    rows = x_sorted.shape[0]
    nblocks = rows // bm
    nf, bf = wgu.shape[1], wgu.shape[4]

    def fe(i, f, be, nu):
        return jnp.where(i < nu[0], f, nf - 1)

    grid_spec = pltpu.PrefetchScalarGridSpec(
        num_scalar_prefetch=2,
        grid=(nblocks, nf),
        in_specs=[
            pl.BlockSpec((bm, D_MODEL), lambda i, f, be, nu: (jnp.minimum(i, nu[0] - 1), 0)),
            pl.BlockSpec((1, 1, 2, D_MODEL, bf), lambda i, f, be, nu: (be[i], fe(i, f, be, nu), 0, 0, 0)),
            pl.BlockSpec((1, bf, D_MODEL), lambda i, f, be, nu: (be[i], fe(i, f, be, nu), 0)),
            pl.BlockSpec((1, 1, 1, 2 * bf), lambda i, f, be, nu: (be[i], fe(i, f, be, nu), 0, 0)),
            pl.BlockSpec((1, 1, D_MODEL), lambda i, f, be, nu: (be[i], 0, 0)),
        ],
        out_specs=pl.BlockSpec((bm, D_MODEL), lambda i, f, be, nu: (i, 0)),
    )
    return pl.pallas_call(
        functools.partial(_moe_kernel, bf=bf),
        grid_spec=grid_spec,
        out_shape=jax.ShapeDtypeStruct((rows, D_MODEL), F32),
        compiler_params=_cparams(("arbitrary", "arbitrary")),
        name="moe_experts",
    )(block_e, n_used, x_sorted, wgu, wd, bgu, bd)


def _row_copy(src_ref, src_row, dst_ref, dst_row, sem):
    return pltpu.make_async_copy(src_ref.at[pl.ds(src_row, 1), :], dst_ref.at[pl.ds(dst_row, 1), :], sem)


def _dispatch_kernel(pe_ref, dest_ref, xn_ref, wgu_ref, perm_ref, wd_ref, xs_ref, wgu_out_ref, wd_out_ref,
                     zero_ref, sem, zsem, *, bm, bm_moe, n_experts, n_tok_steps, n_w_steps):
    step = pl.program_id(0)

    @pl.when(step < n_w_steps)
    def _():
        res = _dot(wgu_ref[0].astype(BF16), perm_ref[...]).astype(BF16)
        wgu_out_ref[0, 0, 0] = res[:, :DEINT // 2]
        wgu_out_ref[0, 0, 1] = res[:, DEINT // 2:]
        wd_out_ref[...] = wd_ref[...].astype(BF16)

    @pl.when(step == 0)
    def _():
        zero_ref[...] = jnp.zeros_like(zero_ref)

        def last_block(e):
            start = pl.multiple_of(pe_ref[e + 1] - bm_moe, bm_moe)
            return pltpu.make_async_copy(zero_ref, xs_ref.at[pl.ds(start, bm_moe), :], zsem)

        for e in range(n_experts):
            @pl.when(pe_ref[e + 1] > pe_ref[e])
            def _():
                last_block(e).start()
        for e in range(n_experts):
            @pl.when(pe_ref[e + 1] > pe_ref[e])
            def _():
                last_block(e).wait()

        def unused_block(b, carry):
            cp = pltpu.make_async_copy(zero_ref, xs_ref.at[pl.ds(pl.multiple_of(b * bm_moe, bm_moe), bm_moe), :], zsem)
            cp.start()
            cp.wait()
            return carry
        lax.fori_loop(pe_ref[n_experts] // bm_moe, xs_ref.shape[0] // bm_moe, unused_block, 0)

    def issue(r, carry):
        for k in range(TOP_K):
            _row_copy(xn_ref, r, xs_ref, dest_ref[TOP_K * r + k], sem).start(priority=k % 2)
        return carry

    def drain(r, carry):
        for k in range(TOP_K):
            _row_copy(xn_ref, r, xs_ref, dest_ref[TOP_K * r + k], sem).wait()
        return carry

    @pl.when(step < n_tok_steps)
    def _():
        lax.fori_loop(0, bm, issue, 0, unroll=8)
        lax.fori_loop(0, bm, drain, 0, unroll=8)


def _dispatch_and_prep(xn, dest, pad_edges, rows, w_gate_up, w_down, bm, bm_moe, bf):
    t = xn.shape[0]
    n_e, d, two_f = w_gate_up.shape
    d_ff = two_f // 2
    half = DEINT // 2
    per_e = two_f // DEINT
    per_f = bf // half
    assert d_ff // WD_ROWS == per_e
    n_w = n_e * per_e
    n_t = t // bm
    r = jnp.arange(DEINT)[:, None]
    c = jnp.arange(DEINT)[None, :]
    perm = jnp.where(c < half, r == 2 * c, r == 2 * (c - half) + 1).astype(BF16)

    def tok(i):
        return jnp.minimum(i, n_t - 1)

    def wstep(i):
        return jnp.minimum(i, n_w - 1)

    grid_spec = pltpu.PrefetchScalarGridSpec(
        num_scalar_prefetch=1,
        grid=(max(n_t, n_w),),
        in_specs=[pl.BlockSpec((bm * TOP_K,), lambda i, pe: (tok(i),), memory_space=pltpu.SMEM),
                  pl.BlockSpec((bm, D_MODEL), lambda i, pe: (tok(i), 0)),
                  pl.BlockSpec((1, d, DEINT), lambda i, pe: (wstep(i) // per_e, 0, wstep(i) % per_e)),
                  pl.BlockSpec((DEINT, DEINT), lambda i, pe: (0, 0)),
                  pl.BlockSpec((1, WD_ROWS, d), lambda i, pe: (wstep(i) // per_e, wstep(i) % per_e, 0))],
        out_specs=[pl.BlockSpec(memory_space=pl.ANY),
                   pl.BlockSpec((1, 1, 2, d, half),
                                lambda i, pe: (wstep(i) // per_e, (wstep(i) % per_e) // per_f, 0, 0, wstep(i) % per_f)),
                   pl.BlockSpec((1, WD_ROWS, d), lambda i, pe: (wstep(i) // per_e, wstep(i) % per_e, 0))],
        scratch_shapes=[pltpu.VMEM((bm_moe, D_MODEL), F32),
                        pltpu.SemaphoreType.DMA(()), pltpu.SemaphoreType.DMA(())],
    )
    return pl.pallas_call(
        functools.partial(_dispatch_kernel, bm=bm, bm_moe=bm_moe, n_experts=n_e, n_tok_steps=n_t, n_w_steps=n_w),
        grid_spec=grid_spec,
        out_shape=[jax.ShapeDtypeStruct((rows, D_MODEL), F32),
                   jax.ShapeDtypeStruct((n_e, d_ff // bf, 2, d, bf), BF16),
                   jax.ShapeDtypeStruct((n_e, d_ff, d), BF16)],
        compiler_params=_cparams(("arbitrary",)),
        name="dispatch_rows_prep_weights",
    )(pad_edges, dest, xn, w_gate_up, perm, w_down)


def _combine_kernel(dcur_ref, dnxt_ref, x1_ref, gate_ref, fw_ref, ys_ref, outa_ref, outb_ref, ybuf, sems,
                    *, bm, nb_a):
    i = pl.program_id(0)
    nb = pl.num_programs(0)
    slot = lax.rem(i, 2)

    def issue(dref, s):
        def body(r, carry):
            for k in range(TOP_K):
                _row_copy(ys_ref, dref[TOP_K * r + k], ybuf.at[s, k], r, sems.at[s]).start(priority=k % 2)
            return carry
        lax.fori_loop(0, bm, body, 0, unroll=8)

    @pl.when(i == 0)
    def _():
        issue(dcur_ref, 0)

    @pl.when(i + 1 < nb)
    def _():
        issue(dnxt_ref, 1 - slot)

    def drain(r, carry):
        for k in range(TOP_K):
            _row_copy(ys_ref, dcur_ref[TOP_K * r + k], ybuf.at[slot, k], r, sems.at[slot]).wait()
        return carry
    lax.fori_loop(0, bm, drain, 0, unroll=8)

    gate = gate_ref[...]
    y = (ybuf[slot, 0] * gate[:, 0:1] + ybuf[slot, 1] * gate[:, 1:2]
         + ybuf[slot, 2] * gate[:, 2:3] + ybuf[slot, 3] * gate[:, 3:4])
    x2 = x1_ref[...] + y
    y_out = x2 * lax.rsqrt(jnp.mean(x2 * x2, axis=-1, keepdims=True) + EPS) * fw_ref[...]

    @pl.when(i < nb_a)
    def _():
        outa_ref[...] = y_out

    @pl.when(i >= nb_a)
    def _():
        outb_ref[...] = y_out


def _combine(x1, ys, dest, gates, final_w, bm, rows_a):
    t = x1.shape[0]
    nb = t // bm
    nb_a = rows_a // bm
    blk = pl.BlockSpec((bm, D_MODEL), lambda i: (i, 0))
    return pl.pallas_call(
        functools.partial(_combine_kernel, bm=bm, nb_a=nb_a),
        grid=(nb,),
        in_specs=[pl.BlockSpec((bm * TOP_K,), lambda i: (i,), memory_space=pltpu.SMEM),
                  pl.BlockSpec((bm * TOP_K,), lambda i: (jnp.minimum(i + 1, nb - 1),), memory_space=pltpu.SMEM),
                  blk,
                  pl.BlockSpec((bm, TOP_K), lambda i: (i, 0)),
                  pl.BlockSpec((1, D_MODEL), lambda i: (0, 0)),
                  pl.BlockSpec(memory_space=pl.ANY)],
        out_specs=_row_specs(bm, nb_a, D_MODEL),
        out_shape=[jax.ShapeDtypeStruct((rows_a, D_MODEL), F32),
                   jax.ShapeDtypeStruct((t - rows_a, D_MODEL), F32)],
        scratch_shapes=[pltpu.VMEM((2, TOP_K, bm, D_MODEL), F32), pltpu.SemaphoreType.DMA((2,))],
        compiler_params=_cparams(("arbitrary",)),
        name="combine_final_norm",
    )(dest, dest, x1, gates, final_w, ys)


def _route(top_idx, rank, counts, bm):
    n_experts = counts.shape[0]
    n_assign = top_idx.size
    padded = ((counts + bm - 1) // bm) * bm
    pad_end = jnp.cumsum(padded)
    pad_start = pad_end - padded
    dest = (jnp.take(pad_start, top_idx) + rank).astype(jnp.int32).reshape(n_assign)
    nblocks = n_assign // bm + n_experts
    n_used = (pad_end[-1] // bm).astype(jnp.int32).reshape(1)
    block_start = jnp.arange(nblocks, dtype=jnp.int32) * bm
    block_e = jnp.minimum(jnp.searchsorted(pad_end, block_start, side="right"), n_experts - 1)
    last_e = block_e[jnp.maximum(n_used[0] - 1, 0)]
    block_e = jnp.where(jnp.arange(nblocks) < n_used[0], block_e, last_e).astype(jnp.int32)
    pad_edges = jnp.concatenate([jnp.zeros((1,), jnp.int32), pad_end.astype(jnp.int32)])
    return dest, pad_edges, block_e, n_used, nblocks * bm


def _trunk(xa, xb, seq_lens, norm1_w, w_in, mlstm_gate_b, mlstm_norm_w, gla_w_lr, gla_b_lr, gla_norm_w, w_out,
           norm2_w, router_w, router_b, w_gate_up, b_gate_up, w_down, b_down, final_norm_w,
           *, bm_in=512, bn_in=2048, bt_mix=256, bm_out=256, bm_disp=128, bm_moe=512, bf_moe=1024, bm_comb=256):
    n_e = router_w.shape[1]
    w_main = jnp.concatenate([w_in[:, :3072], w_in[:, 3088:6160]], axis=1).astype(BF16)
    w_small = jnp.zeros((D_MODEL, SMALL_WIDTH), F32)
    w_small = w_small.at[:, 0:16].set(w_in[:, 3072:3088]).at[:, 16:48].set(w_in[:, 6160:6192]).astype(BF16)
    w_kt = w_in[:, 512:1024].T.astype(BF16)
    proj, small, small_t, k_t = _in_proj(xa, xb, norm1_w.reshape(1, D_MODEL), w_main, w_small, w_small.T, w_kt,
                                         bm_in, bn_in)

    hm_f = _mlstm_dir(proj, k_t, small, small_t, mlstm_gate_b, seq_lens, bt_mix, False)
    hm_b = _mlstm_dir(proj, k_t, small, small_t, mlstm_gate_b, seq_lens, bt_mix, True)
    bcum_f, bcum_b = _gla_gates(small, gla_w_lr, gla_b_lr, bt_mix)
    hg_f = _gla_dir(proj, bcum_f, seq_lens, bt_mix, False)
    hg_b = _gla_dir(proj, bcum_b, seq_lens, bt_mix, True)

    x1, xn, top_idx, gates, rank, counts = _out_proj(
        xa, xb, hm_f, hm_b, hg_f, hg_b, proj, mlstm_norm_w.reshape(1, M_WIDTH), gla_norm_w.reshape(1, G_WIDTH),
        w_out.astype(BF16), norm2_w.reshape(1, D_MODEL), _split_hi_lo(router_w.astype(F32)),
        router_b.reshape(1, n_e).astype(F32), bm_out)

    dest, pad_edges, block_e, n_used, rows = _route(top_idx, rank, counts[0].astype(jnp.int32), bm_moe)
    x_sorted, wgu, wd = _dispatch_and_prep(xn, dest, pad_edges, rows, w_gate_up, w_down, bm_disp, bm_moe, bf_moe)
    nf = w_gate_up.shape[2] // (2 * bf_moe)
    bgu = jnp.concatenate([b_gate_up[:, 0::2].reshape(n_e, nf, 1, bf_moe),
                           b_gate_up[:, 1::2].reshape(n_e, nf, 1, bf_moe)], axis=-1).astype(F32)
    ys = _moe_experts(x_sorted, block_e, n_used, wgu, wd, bgu,
                      b_down[:, None, :].astype(F32), bm_moe)
    return _combine(x1, ys, dest, gates, final_norm_w.reshape(1, D_MODEL), bm_comb, xa.shape[0])


def kernel(x_prompt, x_sample, norm1_w, w_in, mlstm_gate_b, mlstm_norm_w, gla_w_lr, gla_b_lr, gla_norm_w,
           w_out, norm2_w, router_w, router_b, w_gate_up, b_gate_up, w_down, b_down, final_norm_w):
    bp, sp, _ = x_prompt.shape
    bs, ss, _ = x_sample.shape
    seq_lens = (sp,) * bp + (ss,) * bs
    ya, yb = _trunk(x_prompt.reshape(bp * sp, D_MODEL), x_sample.reshape(bs * ss, D_MODEL), seq_lens,
                    norm1_w[0], w_in[0], mlstm_gate_b[0], mlstm_norm_w[0], gla_w_lr[0], gla_b_lr[0],
                    gla_norm_w[0], w_out[0], norm2_w[0], router_w[0], router_b[0], w_gate_up[0], b_gate_up[0],
                    w_down[0], b_down[0], final_norm_w)
    return (ya.reshape(bp, sp, D_MODEL), yb.reshape(bs, ss, D_MODEL))
```

```python
import functools

import jax
import jax.numpy as jnp
from jax import lax
from jax.experimental import pallas as pl
from jax.experimental.pallas import tpu as pltpu

F32 = jnp.float32
BF16 = jnp.bfloat16

D_MODEL = 2048
CHUNK = 64
M_HEADS = 4
M_QK = 128
M_V = 256
M_WIDTH = M_HEADS * M_V
GATE_CAP = 15.0
G_HEADS = 4
G_QK = 128
G_V = 256
G_WIDTH = G_HEADS * G_V
G_RANK = 16
G_TAU = 16.0
TOP_K = 4
SWIGLU_LIMIT = 7.0
SWIGLU_ALPHA = 1.702
EPS = 1e-6
MAIN_WIDTH = 2 * M_HEADS * M_QK + 2 * M_WIDTH + 2 * G_HEADS * G_QK + 2 * G_WIDTH
SMALL_WIDTH = 128
NEG_INF = float("-inf")

VMEM_LIMIT = 56 * 1024 * 1024


def _cparams(sem):
    return pltpu.CompilerParams(dimension_semantics=sem, vmem_limit_bytes=VMEM_LIMIT)


def _log_sigmoid(x):
    return jnp.minimum(x, 0.0) - jnp.log1p(jnp.exp(-jnp.abs(x)))


def _split3(x):
    hi = x.astype(BF16)
    r = x - hi.astype(F32)
    mid = r.astype(BF16)
    lo = (r - mid.astype(F32)).astype(BF16)
    return hi, mid, lo


def _dot(a, b):
    return jnp.dot(a, b, preferred_element_type=F32)


def _dot_nt(a, b):
    return lax.dot_general(a, b, (((1,), (1,)), ((), ())), preferred_element_type=F32)


def _dot_tn(a, b):
    return lax.dot_general(a, b, (((0,), (0,)), ((), ())), preferred_element_type=F32)


U32 = jnp.uint32
HALF = D_MODEL // 2
TOP_BITS = 0xFFFF0000


def _pack_rows(x):
    xb = x.astype(BF16).astype(F32)
    hi = pltpu.bitcast(xb[:, :HALF], U32) & U32(TOP_BITS)
    lo = pltpu.bitcast(xb[:, HALF:], U32) >> 16
    return hi | lo


def _unpack_rows(p):
    return pltpu.bitcast(p & U32(TOP_BITS), F32), pltpu.bitcast(p << 16, F32)


def _split_hi_lo(w):
    hi = w.astype(BF16)
    return jnp.concatenate([hi, (w - hi.astype(F32)).astype(BF16)], axis=1)


def _cum_left(tri, x):
    hi, mid, lo = _split3(x)
    return _dot(tri, hi) + _dot(tri, mid) + _dot(tri, lo)


def _cum_right(x, tri):
    hi, mid, lo = _split3(x)
    return _dot(hi, tri) + _dot(mid, tri) + _dot(lo, tri)


def _inproj_kernel(xa_ref, xb_ref, nw_ref, w_ref, ws_ref, wst_ref, wkt_ref, main_ref, small_ref, smallt_ref, kt_ref,
                   hn_ref, *, nb_a):
    @pl.when(pl.program_id(1) == 0)
    def _():
        x = jnp.where(pl.program_id(0) < nb_a, xa_ref[...], xb_ref[...])
        y = x * lax.rsqrt(jnp.mean(x * x, axis=-1, keepdims=True) + EPS) * nw_ref[...]
        hb = y.astype(BF16)
        hn_ref[...] = hb
        small_ref[...] = _dot(hb, ws_ref[...])
        smallt_ref[...] = _dot_nt(wst_ref[...], hb)
        kt_ref[...] = _dot_nt(wkt_ref[...], hb).astype(BF16)

    main_ref[...] = _dot(hn_ref[...], w_ref[...]).astype(BF16)


def _row_specs(bm, nb_a, width):
    return [pl.BlockSpec((bm, width), lambda i, *_: (jnp.minimum(i, nb_a - 1), 0)),
            pl.BlockSpec((bm, width), lambda i, *_: (jnp.maximum(i - nb_a, 0), 0))]


def _in_proj(xa, xb, norm_w, w_main, w_small, w_small_t, w_kt, bm, bn):
    t = xa.shape[0] + xb.shape[0]
    nb_a = xa.shape[0] // bm
    return pl.pallas_call(
        functools.partial(_inproj_kernel, nb_a=nb_a),
        grid=(t // bm, MAIN_WIDTH // bn),
        in_specs=_row_specs(bm, nb_a, D_MODEL) + [
            pl.BlockSpec((1, D_MODEL), lambda i, j: (0, 0)),
            pl.BlockSpec((D_MODEL, bn), lambda i, j: (0, j)),
            pl.BlockSpec((D_MODEL, SMALL_WIDTH), lambda i, j: (0, 0)),
            pl.BlockSpec((SMALL_WIDTH, D_MODEL), lambda i, j: (0, 0)),
            pl.BlockSpec((M_HEADS * M_QK, D_MODEL), lambda i, j: (0, 0)),
        ],
        out_specs=[
            pl.BlockSpec((bm, bn), lambda i, j: (i, j)),
            pl.BlockSpec((bm, SMALL_WIDTH), lambda i, j: (i, 0)),
            pl.BlockSpec((SMALL_WIDTH, bm), lambda i, j: (0, i)),
            pl.BlockSpec((M_HEADS * M_QK, bm), lambda i, j: (0, i)),
        ],
        out_shape=[
            jax.ShapeDtypeStruct((t, MAIN_WIDTH), BF16),
            jax.ShapeDtypeStruct((t, SMALL_WIDTH), F32),
            jax.ShapeDtypeStruct((SMALL_WIDTH, t), F32),
            jax.ShapeDtypeStruct((M_HEADS * M_QK, t), BF16),
        ],
        scratch_shapes=[pltpu.VMEM((bm, D_MODEL), BF16)],
        compiler_params=_cparams(("arbitrary", "arbitrary")),
        name="in_proj",
    )(xa, xb, norm_w, w_main, w_small, w_small_t, w_kt)


def _chunk_masks(reverse):
    row = lax.broadcasted_iota(jnp.int32, (CHUNK, CHUNK), 0)
    col = lax.broadcasted_iota(jnp.int32, (CHUNK, CHUNK), 1)
    return (col >= row) if reverse else (col <= row)


def _reset_pred(blk, reset_blocks):
    pred = blk == reset_blocks[0]
    for rb in reset_blocks[1:]:
        pred = jnp.logical_or(pred, blk == rb)
    return pred


def _mlstm_kernel(q_ref, k_ref, kt_ref, v_ref, sm_ref, smt_ref, gb_ref, gbt_ref, tri_ref, trit_ref,
                  h_ref, c_ref, n_ref, m_ref, *, reverse, nblocks, reset_blocks, bt):
    i = pl.program_id(0)
    blk = (nblocks - 1 - i) if reverse else i

    @pl.when(_reset_pred(blk, reset_blocks))
    def _():
        c_ref[...] = jnp.zeros_like(c_ref)
        n_ref[...] = jnp.zeros_like(n_ref)
        m_ref[...] = jnp.zeros_like(m_ref)

    d = 1 if reverse else 0
    scale = M_QK ** -0.5
    gates = GATE_CAP * jnp.tanh((sm_ref[...] + gb_ref[...]) / GATE_CAP)
    bcum = _cum_left(tri_ref[...], _log_sigmoid(gates))
    gates_t = GATE_CAP * jnp.tanh((smt_ref[0:16, :] + gbt_ref[...]) / GATE_CAP)
    bcum_t = _cum_right(_log_sigmoid(gates_t), trit_ref[...])
    i_row = gates_t[8 * d:8 * d + 4, :]
    b_row = bcum_t[8 * d + 4:8 * d + 8, :]
    last = 0 if reverse else bt - 1
    g4 = b_row[:, last:last + 1]
    w4 = g4 - b_row + i_row
    mloc4 = jnp.max(w4, axis=1, keepdims=True)
    e4 = jnp.exp(w4 - mloc4)
    d4 = i_row - b_row
    row = lax.broadcasted_iota(jnp.int32, (bt, bt), 0)
    col = lax.broadcasted_iota(jnp.int32, (bt, bt), 1)
    mask = (col >= row) if reverse else (col <= row)
    ones_b = jnp.ones((bt, 128), BF16)
    e4b = e4.astype(BF16)
    outs = []
    for h in range(M_HEADS):
        fl = 8 * d + 4 + h
        g, m_loc = g4[h:h + 1], mloc4[h:h + 1]
        qb = q_ref[:, h * M_QK:(h + 1) * M_QK]
        kb = k_ref[:, h * M_QK:(h + 1) * M_QK]
        ktb = kt_ref[h * M_QK:(h + 1) * M_QK, :]
        vb = v_ref[:, h * M_V:(h + 1) * M_V]
        ekt = (ktb.astype(F32) * e4[h:h + 1]).astype(BF16)
        kv_loc = _dot(ekt, vb)
        n_loc = _dot(e4b, kb)[h:h + 1]
        s = _dot(qb, ktb) * scale
        b_b = jnp.broadcast_to(bcum[:, fl:fl + 1], (bt, 128))
        a_intra = jnp.where(mask, jnp.tile(b_b, (1, bt // 128)) + d4[h:h + 1], NEG_INF)
        m_intra = jnp.max(a_intra, axis=1, keepdims=True)
        pb = (jnp.exp(a_intra - m_intra) * s).astype(BF16)
        pv = _dot(pb, vb)
        p_sum = _dot(pb, ones_b)

        m_in, c_in, n_in = m_ref[h], c_ref[h], n_ref[h]
        m_intra_b = jnp.broadcast_to(m_intra, (bt, 128))
        a_inter = b_b + m_in
        m_row = jnp.maximum(m_intra_b, a_inter)
        f_intra = jnp.exp(m_intra_b - m_row)
        f_inter = jnp.exp(a_inter - m_row)
        qn = _dot_nt(qb, jnp.broadcast_to(n_in, (128, M_QK)).astype(BF16)) * scale
        den = f_intra * p_sum + f_inter * qn
        rinv = 1.0 / jnp.maximum(jnp.abs(den), jnp.exp(-m_row))
        qc = _dot(qb, c_in.astype(BF16)) * scale
        w_intra = jnp.tile(f_intra * rinv, (1, M_V // 128))
        w_inter = jnp.tile(f_inter * rinv, (1, M_V // 128))
        outs.append((w_intra * pv + w_inter * qc).astype(h_ref.dtype))

        m_new = jnp.maximum(g + m_in, m_loc)
        aa = jnp.exp(g + m_in - m_new)
        bb = jnp.exp(m_loc - m_new)
        c_ref[h] = aa * c_in + bb * kv_loc
        n_ref[h] = aa * n_in + bb * n_loc
        m_ref[h] = m_new
    h_ref[...] = jnp.concatenate(outs, axis=1)


def _gla_gate_kernel(sm_ref, wlr_ref, blr_ref, trif_ref, trib_ref, bf_ref, bb_ref):
    w = G_HEADS * G_QK
    sm = sm_ref[...]
    s_hi = sm.astype(BF16)
    s_lo = (sm - s_hi.astype(F32)).astype(BF16)
    r_hi = _dot(s_hi, wlr_ref[...])
    z = r_hi[:, :2 * w] + r_hi[:, 2 * w:] + _dot(s_lo, wlr_ref[:, :2 * w]) + blr_ref[...]
    log_a = _log_sigmoid(z) * (1.0 / G_TAU)
    bf_ref[...] = _cum_left(trif_ref[...], log_a[:, :w])
    bb_ref[...] = _cum_left(trib_ref[...], log_a[:, w:])


def _gla_gates(small, w_lr, b_lr, bt):
    t = small.shape[0]
    w = G_HEADS * G_QK
    wlr = jnp.zeros((SMALL_WIDTH, 2 * w), F32)
    wlr = wlr.at[16:32, :w].set(w_lr[0].astype(F32)).at[32:48, w:].set(w_lr[1].astype(F32))
    blr = jnp.concatenate([b_lr[0], b_lr[1]]).astype(F32).reshape(1, 2 * w)
    out = pl.BlockSpec((bt, w), lambda i: (i, 0))
    return pl.pallas_call(
        _gla_gate_kernel,
        grid=(t // bt,),
        in_specs=[pl.BlockSpec((bt, SMALL_WIDTH), lambda i: (i, 0)),
                  pl.BlockSpec((SMALL_WIDTH, 4 * w), lambda i: (0, 0)),
                  pl.BlockSpec((1, 2 * w), lambda i: (0, 0)),
                  pl.BlockSpec((bt, bt), lambda i: (0, 0)),
                  pl.BlockSpec((bt, bt), lambda i: (0, 0))],
        out_specs=[out, out],
        out_shape=[jax.ShapeDtypeStruct((t, w), F32), jax.ShapeDtypeStruct((t, w), F32)],
        compiler_params=_cparams(("arbitrary",)),
        name="gla_gates",
    )(small, _split_hi_lo(wlr), blr, _block_tri(bt, False), _block_tri(bt, True))


def _gla_kernel(q_ref, k_ref, v_ref, b_ref, o_ref, s_ref, *, reverse, nblocks, reset_blocks, bt):
    i = pl.program_id(0)
    blk = (nblocks - 1 - i) if reverse else i

    @pl.when(_reset_pred(blk, reset_blocks))
    def _():
        s_ref[...] = jnp.zeros_like(s_ref)

    scale = G_QK ** -0.5
    bcum = b_ref[...]
    mask = _chunk_masks(reverse)
    nchunks = bt // CHUNK
    order = range(nchunks - 1, -1, -1) if reverse else range(nchunks)
    sts = [s_ref[h] for h in range(G_HEADS)]
    for c in order:
        r0 = c * CHUNK
        outs = []
        for h in range(G_HEADS):
            b = bcum[r0:r0 + CHUNK, h * G_QK:(h + 1) * G_QK]
            g = b[0:1] if reverse else b[CHUNK - 1:CHUNK]
            qf = q_ref[r0:r0 + CHUNK, h * G_QK:(h + 1) * G_QK].astype(F32)
            kf = k_ref[r0:r0 + CHUNK, h * G_QK:(h + 1) * G_QK].astype(F32)
            vb = v_ref[r0:r0 + CHUNK, h * G_V:(h + 1) * G_V]
            q_dec = (qf * scale * jnp.exp(b)).astype(BF16)
            k_inv = (kf * jnp.exp(-b)).astype(BF16)
            k_end = (kf * jnp.exp(g - b)).astype(BF16)
            att = jnp.where(mask, _dot_nt(q_dec, k_inv), 0.0)
            o = _dot(att.astype(BF16), vb) + _dot_nt(q_dec, sts[h].astype(BF16))
            outs.append(o.astype(o_ref.dtype))
            sts[h] = sts[h] * jnp.exp(g) + _dot_tn(vb, k_end)
        o_ref[r0:r0 + CHUNK, :] = jnp.concatenate(outs, axis=1)
    for h in range(G_HEADS):
        s_ref[h] = sts[h]


def _block_tri(bt, upper, chunk=CHUNK):
    r = jnp.arange(bt)
    same = (r[:, None] // chunk) == (r[None, :] // chunk)
    tri = (r[None, :] >= r[:, None]) if upper else (r[None, :] <= r[:, None])
    return (same & tri).astype(BF16)


def _reset_blocks(seq_lens, bt, reverse):
    out, off = [], 0
    for s in seq_lens:
        assert s % bt == 0
        out.append((off + s) // bt - 1 if reverse else off // bt)
        off += s
    return tuple(out)


def _mlstm_dir(proj, k_t, small, small_t, gate_b, seq_lens, bt, reverse):
    t = proj.shape[0]
    nb = t // bt
    tri = _block_tri(bt, reverse, chunk=bt)
    rev = (lambda i: nb - 1 - i) if reverse else (lambda i: i)
    gb = jnp.zeros((1, SMALL_WIDTH), F32).at[0, :16].set(gate_b.reshape(16))
    gbt = gate_b.reshape(16, 1)
    kern = functools.partial(_mlstm_kernel, reverse=reverse, nblocks=nb,
                             reset_blocks=_reset_blocks(seq_lens, bt, reverse), bt=bt)
    return pl.pallas_call(
        kern,
        grid=(nb,),
        in_specs=[
            pl.BlockSpec((bt, 512), lambda i: (rev(i), 0)),
            pl.BlockSpec((bt, 512), lambda i: (rev(i), 1)),
            pl.BlockSpec((M_HEADS * M_QK, bt), lambda i: (0, rev(i))),
            pl.BlockSpec((bt, 1024), lambda i: (rev(i), 1)),
            pl.BlockSpec((bt, SMALL_WIDTH), lambda i: (rev(i), 0)),
            pl.BlockSpec((SMALL_WIDTH, bt), lambda i: (0, rev(i))),
            pl.BlockSpec((1, SMALL_WIDTH), lambda i: (0, 0)),
            pl.BlockSpec((16, 1), lambda i: (0, 0)),
            pl.BlockSpec((bt, bt), lambda i: (0, 0)),
            pl.BlockSpec((bt, bt), lambda i: (0, 0)),
        ],
        out_specs=pl.BlockSpec((bt, M_WIDTH), lambda i: (rev(i), 0)),
        out_shape=jax.ShapeDtypeStruct((t, M_WIDTH), BF16),
        scratch_shapes=[pltpu.VMEM((M_HEADS, M_QK, M_V), F32),
                        pltpu.VMEM((M_HEADS, 1, M_QK), F32),
                        pltpu.VMEM((M_HEADS, 1, 1), F32)],
        compiler_params=_cparams(("arbitrary",)),
        name="mlstm_bwd" if reverse else "mlstm_fwd",
    )(proj, proj, k_t, proj, small, small_t, gb, gbt, tri, tri.T)


def _gla_dir(proj, bcum, seq_lens, bt, reverse):
    t = proj.shape[0]
    nb = t // bt
    rev = (lambda i: nb - 1 - i) if reverse else (lambda i: i)
    kern = functools.partial(_gla_kernel, reverse=reverse, nblocks=nb,
                             reset_blocks=_reset_blocks(seq_lens, bt, reverse), bt=bt)
    return pl.pallas_call(
        kern,
        grid=(nb,),
        in_specs=[
            pl.BlockSpec((bt, 512), lambda i: (rev(i), 6)),
            pl.BlockSpec((bt, 512), lambda i: (rev(i), 7)),
            pl.BlockSpec((bt, 1024), lambda i: (rev(i), 4)),
            pl.BlockSpec((bt, G_HEADS * G_QK), lambda i: (rev(i), 0)),
        ],
        out_specs=pl.BlockSpec((bt, G_WIDTH), lambda i: (rev(i), 0)),
        out_shape=jax.ShapeDtypeStruct((t, G_WIDTH), BF16),
        scratch_shapes=[pltpu.VMEM((G_HEADS, G_V, G_QK), F32)],
        compiler_params=_cparams(("arbitrary",)),
        name="gla_bwd" if reverse else "gla_fwd",
    )(proj, proj, proj, bcum)


def _head_norm(hsum, w, nheads, hd):
    outs = []
    for h in range(nheads):
        x = hsum[:, h * hd:(h + 1) * hd]
        outs.append(x * lax.rsqrt(jnp.mean(x * x, axis=-1, keepdims=True) + EPS))
    return jnp.concatenate(outs, axis=-1) * w


def _outproj_kernel(xa_ref, xb_ref, mf_ref, mb_ref, gf_ref, gb_ref, mo_ref, gr_ref, mnw_ref, gnw_ref, wout_ref,
                    n2w_ref, rw_ref, rb_ref, lt_ref, x1_ref, xn_ref, idx_ref, gate_ref, rank_ref, cnt_ref,
                    *, n_experts, nb_a):
    hm = mf_ref[...].astype(F32) + mb_ref[...].astype(F32)
    a_out = _head_norm(hm, mnw_ref[...], M_HEADS, M_V) * jax.nn.sigmoid(mo_ref[...].astype(F32))
    hg = gf_ref[...].astype(F32) + gb_ref[...].astype(F32)
    r = gr_ref[...].astype(F32)
    b_out = _head_norm(hg, gnw_ref[...], G_HEADS, G_V) * (r * jax.nn.sigmoid(r))
    mixed = jnp.concatenate([a_out, b_out], axis=-1).astype(BF16)
    x = jnp.where(pl.program_id(0) < nb_a, xa_ref[...], xb_ref[...])
    x1 = x + _dot(mixed, wout_ref[...])
    x1_ref[...] = x1
    xn = x1 * lax.rsqrt(jnp.mean(x1 * x1, axis=-1, keepdims=True) + EPS) * n2w_ref[...]
    xn_ref[...] = _pack_rows(xn)
    x_hi = xn.astype(BF16)
    x_lo = (xn - x_hi.astype(F32)).astype(BF16)
    r_hi = _dot(x_hi, rw_ref[...])
    logits = (r_hi[:, :n_experts] + r_hi[:, n_experts:] + _dot(x_lo, rw_ref[:, :n_experts])
              + rb_ref[...])
    lane = lax.broadcasted_iota(jnp.int32, logits.shape, 1).astype(F32)
    vals, idxs = [], []
    cur = logits
    for _ in range(TOP_K):
        m = jnp.max(cur, axis=1, keepdims=True)
        sel = jnp.min(jnp.where(cur == m, lane, float(n_experts)), axis=1, keepdims=True)
        vals.append(m)
        idxs.append(sel)
        cur = jnp.where(lane == sel, NEG_INF, cur)
    exps = [jnp.exp(v - vals[0]) for v in vals]
    tot = exps[0] + exps[1] + exps[2] + exps[3]
    gate_ref[...] = jnp.concatenate(exps, axis=1) / tot
    idx_ref[...] = jnp.concatenate(idxs, axis=1).astype(jnp.int32)

    @pl.when(pl.program_id(0) == 0)
    def _():
        cnt_ref[...] = jnp.zeros_like(cnt_ref)

    hits = [lane == sel for sel in idxs]
    tok_oh = jnp.where(hits[0] | hits[1] | hits[2] | hits[3], 1.0, 0.0)
    before = _dot(lt_ref[...], tok_oh.astype(BF16)) + cnt_ref[...]
    rank_ref[...] = jnp.concatenate(
        [jnp.sum(jnp.where(hit, before, 0.0), axis=1, keepdims=True) for hit in hits], axis=1).astype(jnp.int32)
    cnt_ref[...] += jnp.sum(tok_oh, axis=0, keepdims=True)


def _out_proj(xa, xb, hm_f, hm_b, hg_f, hg_b, proj, mnw, gnw, w_out, n2w, router_w, router_b, bm):
    t = xa.shape[0] + xb.shape[0]
    nb_a = xa.shape[0] // bm
    n_e = router_w.shape[1] // 2
    row = lambda c: pl.BlockSpec((bm, 1024), lambda i: (i, c))
    full = lambda a: pl.BlockSpec(a.shape, lambda i: (0,) * a.ndim)
    r = jnp.arange(bm)
    lower = (r[None, :] < r[:, None]).astype(BF16)
    args = (xa, xb, hm_f, hm_b, hg_f, hg_b, proj, proj, mnw, gnw, w_out, n2w, router_w, router_b, lower)
    return pl.pallas_call(
        functools.partial(_outproj_kernel, n_experts=n_e, nb_a=nb_a),
        grid=(t // bm,),
        in_specs=_row_specs(bm, nb_a, D_MODEL) + [row(0), row(0), row(0), row(0), row(2), row(5)]
        + [full(a) for a in args[8:]],
        out_specs=[pl.BlockSpec((bm, D_MODEL), lambda i: (i, 0)),
                   pl.BlockSpec((bm, HALF), lambda i: (i, 0)),
                   pl.BlockSpec((bm, TOP_K), lambda i: (i, 0)),
                   pl.BlockSpec((bm, TOP_K), lambda i: (i, 0)),
                   pl.BlockSpec((bm, TOP_K), lambda i: (i, 0)),
                   pl.BlockSpec((1, n_e), lambda i: (0, 0))],
        out_shape=[jax.ShapeDtypeStruct((t, D_MODEL), F32),
                   jax.ShapeDtypeStruct((t, HALF), U32),
                   jax.ShapeDtypeStruct((t, TOP_K), jnp.int32),
                   jax.ShapeDtypeStruct((t, TOP_K), F32),
                   jax.ShapeDtypeStruct((t, TOP_K), jnp.int32),
                   jax.ShapeDtypeStruct((1, n_e), F32)],
        compiler_params=_cparams(("arbitrary",)),
        name="out_proj_router",
    )(*args)


DEINT = 512
WD_ROWS = 256


def _moe_kernel(be_ref, nu_ref, x_ref, wgu_ref, wd_ref, bgu_ref, bd_ref, out_ref, acc_ref, *, bf, nf):
    i = pl.program_id(0)
    f = pl.program_id(1)
    used = i < nu_ref[0]

    @pl.when(jnp.logical_and(jnp.logical_not(used), f == 0))
    def _():
        out_ref[...] = jnp.zeros_like(out_ref)

    @pl.when(used)
    def _():
        x_hi, x_lo = _unpack_rows(x_ref[...])
        xb = jnp.concatenate([x_hi.astype(BF16), x_lo.astype(BF16)], axis=1)
        g = jnp.minimum(_dot(xb, wgu_ref[0, 0, 0]) + bgu_ref[0, 0, :, :bf], SWIGLU_LIMIT)
        u = jnp.clip(_dot(xb, wgu_ref[0, 0, 1]) + bgu_ref[0, 0, :, bf:], -SWIGLU_LIMIT, SWIGLU_LIMIT)
        act = ((u + 1.0) * g * jax.nn.sigmoid(g * SWIGLU_ALPHA)).astype(BF16)
        contrib = _dot(act, wd_ref[0])

        @pl.when(f == 0)
        def _():
            acc_ref[...] = contrib + bd_ref[0]

        @pl.when(f != 0)
        def _():
            acc_ref[...] += contrib

        @pl.when(f == nf - 1)
        def _():
            out_ref[...] = _pack_rows(acc_ref[...])


def _moe_experts(x_sorted, block_e, n_used, wgu, wd, bgu, bd, bm):
    rows = x_sorted.shape[0]
    nblocks = rows // bm
    nf, bf = wgu.shape[1], wgu.shape[4]

    def fe(i, f, be, nu):
        return jnp.where(i < nu[0], f, nf - 1)

    grid_spec = pltpu.PrefetchScalarGridSpec(
        num_scalar_prefetch=2,
        grid=(nblocks, nf),
        in_specs=[
            pl.BlockSpec((bm, HALF), lambda i, f, be, nu: (jnp.minimum(i, nu[0] - 1), 0)),
            pl.BlockSpec((1, 1, 2, D_MODEL, bf), lambda i, f, be, nu: (be[i], fe(i, f, be, nu), 0, 0, 0)),
            pl.BlockSpec((1, bf, D_MODEL), lambda i, f, be, nu: (be[i], fe(i, f, be, nu), 0)),
            pl.BlockSpec((1, 1, 1, 2 * bf), lambda i, f, be, nu: (be[i], fe(i, f, be, nu), 0, 0)),
            pl.BlockSpec((1, 1, D_MODEL), lambda i, f, be, nu: (be[i], 0, 0)),
        ],
        out_specs=pl.BlockSpec((bm, HALF), lambda i, f, be, nu: (i, 0)),
        scratch_shapes=[pltpu.VMEM((bm, D_MODEL), F32)],
    )
    return pl.pallas_call(
        functools.partial(_moe_kernel, bf=bf, nf=nf),
        grid_spec=grid_spec,
        out_shape=jax.ShapeDtypeStruct((rows, HALF), U32),
        compiler_params=_cparams(("arbitrary", "arbitrary")),
        name="moe_experts",
    )(block_e, n_used, x_sorted, wgu, wd, bgu, bd)


def _row_copy(src_ref, src_row, dst_ref, dst_row, sem):
    return pltpu.make_async_copy(src_ref.at[pl.ds(src_row, 1), :], dst_ref.at[pl.ds(dst_row, 1), :], sem)


def _dispatch_kernel(pe_ref, dest_ref, xn_ref, wgu_ref, perm_ref, wd_ref, xs_ref, wgu_out_ref, wd_out_ref,
                     zero_ref, sem, zsem, *, bm, bm_moe, n_experts, n_tok_steps, n_w_steps):
    step = pl.program_id(0)

    @pl.when(step < n_w_steps)
    def _():
        res = _dot(wgu_ref[0].astype(BF16), perm_ref[...]).astype(BF16)
        wgu_out_ref[0, 0, 0] = res[:, :DEINT // 2]
        wgu_out_ref[0, 0, 1] = res[:, DEINT // 2:]
        wd_out_ref[...] = wd_ref[...].astype(BF16)

    @pl.when(step == 0)
    def _():
        zero_ref[...] = jnp.zeros_like(zero_ref)

        def last_block(e):
            start = pl.multiple_of(pe_ref[e + 1] - bm_moe, bm_moe)
            return pltpu.make_async_copy(zero_ref, xs_ref.at[pl.ds(start, bm_moe), :], zsem)

        for e in range(n_experts):
            @pl.when(pe_ref[e + 1] > pe_ref[e])
            def _():
                last_block(e).start()
        for e in range(n_experts):
            @pl.when(pe_ref[e + 1] > pe_ref[e])
            def _():
                last_block(e).wait()

        def unused_block(b, carry):
            cp = pltpu.make_async_copy(zero_ref, xs_ref.at[pl.ds(pl.multiple_of(b * bm_moe, bm_moe), bm_moe), :], zsem)
            cp.start()
            cp.wait()
            return carry
        lax.fori_loop(pe_ref[n_experts] // bm_moe, xs_ref.shape[0] // bm_moe, unused_block, 0)

    def issue(r, carry):
        for k in range(TOP_K):
            _row_copy(xn_ref, r, xs_ref, dest_ref[TOP_K * r + k], sem).start(priority=k % 2)
        return carry

    def drain(r, carry):
        for k in range(TOP_K):
            _row_copy(xn_ref, r, xs_ref, dest_ref[TOP_K * r + k], sem).wait()
        return carry

    @pl.when(step < n_tok_steps)
    def _():
        lax.fori_loop(0, bm, issue, 0, unroll=8)
        lax.fori_loop(0, bm, drain, 0, unroll=8)


def _dispatch_and_prep(xn, dest, pad_edges, rows, w_gate_up, w_down, bm, bm_moe, bf):
    t = xn.shape[0]
    n_e, d, two_f = w_gate_up.shape
    d_ff = two_f // 2
    half = DEINT // 2
    per_e = two_f // DEINT
    per_f = bf // half
    assert d_ff // WD_ROWS == per_e
    n_w = n_e * per_e
    n_t = t // bm
    r = jnp.arange(DEINT)[:, None]
    c = jnp.arange(DEINT)[None, :]
    perm = jnp.where(c < half, r == 2 * c, r == 2 * (c - half) + 1).astype(BF16)

    def tok(i):
        return jnp.minimum(i, n_t - 1)

    def wstep(i):
        return jnp.minimum(i, n_w - 1)

    grid_spec = pltpu.PrefetchScalarGridSpec(
        num_scalar_prefetch=1,
        grid=(max(n_t, n_w),),
        in_specs=[pl.BlockSpec((bm * TOP_K,), lambda i, pe: (tok(i),), memory_space=pltpu.SMEM),
                  pl.BlockSpec((bm, HALF), lambda i, pe: (tok(i), 0)),
                  pl.BlockSpec((1, d, DEINT), lambda i, pe: (wstep(i) // per_e, 0, wstep(i) % per_e)),
                  pl.BlockSpec((DEINT, DEINT), lambda i, pe: (0, 0)),
                  pl.BlockSpec((1, WD_ROWS, d), lambda i, pe: (wstep(i) // per_e, wstep(i) % per_e, 0))],
        out_specs=[pl.BlockSpec(memory_space=pl.ANY),
                   pl.BlockSpec((1, 1, 2, d, half),
                                lambda i, pe: (wstep(i) // per_e, (wstep(i) % per_e) // per_f, 0, 0, wstep(i) % per_f)),
                   pl.BlockSpec((1, WD_ROWS, d), lambda i, pe: (wstep(i) // per_e, wstep(i) % per_e, 0))],
        scratch_shapes=[pltpu.VMEM((bm_moe, HALF), U32),
                        pltpu.SemaphoreType.DMA(()), pltpu.SemaphoreType.DMA(())],
    )
    return pl.pallas_call(
        functools.partial(_dispatch_kernel, bm=bm, bm_moe=bm_moe, n_experts=n_e, n_tok_steps=n_t, n_w_steps=n_w),
        grid_spec=grid_spec,
        out_shape=[jax.ShapeDtypeStruct((rows, HALF), U32),
                   jax.ShapeDtypeStruct((n_e, d_ff // bf, 2, d, bf), BF16),
                   jax.ShapeDtypeStruct((n_e, d_ff, d), BF16)],
        compiler_params=_cparams(("arbitrary",)),
        name="dispatch_rows_prep_weights",
    )(pad_edges, dest, xn, w_gate_up, perm, w_down)


def _combine_kernel(dcur_ref, dnxt_ref, x1_ref, gate_ref, fw_ref, ys_ref, outa_ref, outb_ref, ybuf, sems,
                    *, bm, nb_a):
    i = pl.program_id(0)
    nb = pl.num_programs(0)
    slot = lax.rem(i, 2)

    def issue(dref, s):
        def body(r, carry):
            for k in range(TOP_K):
                _row_copy(ys_ref, dref[TOP_K * r + k], ybuf.at[s, k], r, sems.at[s]).start(priority=k % 2)
            return carry
        lax.fori_loop(0, bm, body, 0, unroll=8)

    @pl.when(i == 0)
    def _():
        issue(dcur_ref, 0)

    @pl.when(i + 1 < nb)
    def _():
        issue(dnxt_ref, 1 - slot)

    def drain(r, carry):
        for k in range(TOP_K):
            _row_copy(ys_ref, dcur_ref[TOP_K * r + k], ybuf.at[slot, k], r, sems.at[slot]).wait()
        return carry
    lax.fori_loop(0, bm, drain, 0, unroll=8)

    gate = gate_ref[...]
    halves = [_unpack_rows(ybuf[slot, k]) for k in range(TOP_K)]
    y = jnp.concatenate([sum(halves[k][part] * gate[:, k:k + 1] for k in range(TOP_K)) for part in range(2)], axis=1)
    x2 = x1_ref[...] + y
    y_out = x2 * lax.rsqrt(jnp.mean(x2 * x2, axis=-1, keepdims=True) + EPS) * fw_ref[...]

    @pl.when(i < nb_a)
    def _():
        outa_ref[...] = y_out

    @pl.when(i >= nb_a)
    def _():
        outb_ref[...] = y_out


def _combine(x1, ys, dest, gates, final_w, bm, rows_a):
    t = x1.shape[0]
    nb = t // bm
    nb_a = rows_a // bm
    blk = pl.BlockSpec((bm, D_MODEL), lambda i: (i, 0))
    return pl.pallas_call(
        functools.partial(_combine_kernel, bm=bm, nb_a=nb_a),
        grid=(nb,),
        in_specs=[pl.BlockSpec((bm * TOP_K,), lambda i: (i,), memory_space=pltpu.SMEM),
                  pl.BlockSpec((bm * TOP_K,), lambda i: (jnp.minimum(i + 1, nb - 1),), memory_space=pltpu.SMEM),
                  blk,
                  pl.BlockSpec((bm, TOP_K), lambda i: (i, 0)),
                  pl.BlockSpec((1, D_MODEL), lambda i: (0, 0)),
                  pl.BlockSpec(memory_space=pl.ANY)],
        out_specs=_row_specs(bm, nb_a, D_MODEL),
        out_shape=[jax.ShapeDtypeStruct((rows_a, D_MODEL), F32),
                   jax.ShapeDtypeStruct((t - rows_a, D_MODEL), F32)],
        scratch_shapes=[pltpu.VMEM((2, TOP_K, bm, HALF), U32), pltpu.SemaphoreType.DMA((2,))],
        compiler_params=_cparams(("arbitrary",)),
        name="combine_final_norm",
    )(dest, dest, x1, gates, final_w, ys)


def _route(top_idx, rank, counts, bm):
    n_experts = counts.shape[0]
    n_assign = top_idx.size
    padded = ((counts + bm - 1) // bm) * bm
    pad_end = jnp.cumsum(padded)
    pad_start = pad_end - padded
    dest = (jnp.take(pad_start, top_idx) + rank).astype(jnp.int32).reshape(n_assign)
    nblocks = n_assign // bm + n_experts
    n_used = (pad_end[-1] // bm).astype(jnp.int32).reshape(1)
    block_start = jnp.arange(nblocks, dtype=jnp.int32) * bm
    block_e = jnp.minimum(jnp.sum(pad_end[None, :] <= block_start[:, None], axis=1), n_experts - 1)
    last_e = block_e[jnp.maximum(n_used[0] - 1, 0)]
    block_e = jnp.where(jnp.arange(nblocks) < n_used[0], block_e, last_e).astype(jnp.int32)
    pad_edges = jnp.concatenate([jnp.zeros((1,), jnp.int32), pad_end.astype(jnp.int32)])
    return dest, pad_edges, block_e, n_used, nblocks * bm


def _trunk(xa, xb, seq_lens, norm1_w, w_in, mlstm_gate_b, mlstm_norm_w, gla_w_lr, gla_b_lr, gla_norm_w, w_out,
           norm2_w, router_w, router_b, w_gate_up, b_gate_up, w_down, b_down, final_norm_w,
           *, bm_in=512, bn_in=2048, bt_mix=256, bm_out=256, bm_disp=128, bm_moe=512, bf_moe=1024, bm_comb=256):
    n_e = router_w.shape[1]
    w_main = jnp.concatenate([w_in[:, :3072], w_in[:, 3088:6160]], axis=1).astype(BF16)
    w_small = jnp.zeros((D_MODEL, SMALL_WIDTH), F32)
    w_small = w_small.at[:, 0:16].set(w_in[:, 3072:3088]).at[:, 16:48].set(w_in[:, 6160:6192]).astype(BF16)
    w_kt = w_in[:, 512:1024].T.astype(BF16)
    proj, small, small_t, k_t = _in_proj(xa, xb, norm1_w.reshape(1, D_MODEL), w_main, w_small, w_small.T, w_kt,
                                         bm_in, bn_in)

    hm_f = _mlstm_dir(proj, k_t, small, small_t, mlstm_gate_b, seq_lens, bt_mix, False)
    hm_b = _mlstm_dir(proj, k_t, small, small_t, mlstm_gate_b, seq_lens, bt_mix, True)
    bcum_f, bcum_b = _gla_gates(small, gla_w_lr, gla_b_lr, bt_mix)
    hg_f = _gla_dir(proj, bcum_f, seq_lens, bt_mix, False)
    hg_b = _gla_dir(proj, bcum_b, seq_lens, bt_mix, True)

    x1, xn, top_idx, gates, rank, counts = _out_proj(
        xa, xb, hm_f, hm_b, hg_f, hg_b, proj, mlstm_norm_w.reshape(1, M_WIDTH), gla_norm_w.reshape(1, G_WIDTH),
        w_out.astype(BF16), norm2_w.reshape(1, D_MODEL), _split_hi_lo(router_w.astype(F32)),
        router_b.reshape(1, n_e).astype(F32), bm_out)

    dest, pad_edges, block_e, n_used, rows = _route(top_idx, rank, counts[0].astype(jnp.int32), bm_moe)
    x_sorted, wgu, wd = _dispatch_and_prep(xn, dest, pad_edges, rows, w_gate_up, w_down, bm_disp, bm_moe, bf_moe)
    nf = w_gate_up.shape[2] // (2 * bf_moe)
    bgu = jnp.concatenate([b_gate_up[:, 0::2].reshape(n_e, nf, 1, bf_moe),
                           b_gate_up[:, 1::2].reshape(n_e, nf, 1, bf_moe)], axis=-1).astype(F32)
    ys = _moe_experts(x_sorted, block_e, n_used, wgu, wd, bgu,
                      b_down[:, None, :].astype(F32), bm_moe)
    return _combine(x1, ys, dest, gates, final_norm_w.reshape(1, D_MODEL), bm_comb, xa.shape[0])


def kernel(x_prompt, x_sample, norm1_w, w_in, mlstm_gate_b, mlstm_norm_w, gla_w_lr, gla_b_lr, gla_norm_w,
           w_out, norm2_w, router_w, router_b, w_gate_up, b_gate_up, w_down, b_down, final_norm_w):
    bp, sp, _ = x_prompt.shape
    bs, ss, _ = x_sample.shape
    seq_lens = (sp,) * bp + (ss,) * bs
    ya, yb = _trunk(x_prompt.reshape(bp * sp, D_MODEL), x_sample.reshape(bs * ss, D_MODEL), seq_lens,
                    norm1_w[0], w_in[0], mlstm_gate_b[0], mlstm_norm_w[0], gla_w_lr[0], gla_b_lr[0],
                    gla_norm_w[0], w_out[0], norm2_w[0], router_w[0], router_b[0], w_gate_up[0], b_gate_up[0],
                    w_down[0], b_down[0], final_norm_w)
    return (ya.reshape(bp, sp, D_MODEL), yb.reshape(bs, ss, D_MODEL))
```

```python
import functools

import jax
import jax.numpy as jnp
from jax import lax
from jax.experimental import pallas as pl
from jax.experimental.pallas import tpu as pltpu

F32 = jnp.float32
BF16 = jnp.bfloat16

D_MODEL = 2048
CHUNK = 64
M_HEADS = 4
M_QK = 128
M_V = 256
M_WIDTH = M_HEADS * M_V
GATE_CAP = 15.0
G_HEADS = 4
G_QK = 128
G_V = 256
G_WIDTH = G_HEADS * G_V
G_RANK = 16
G_TAU = 16.0
TOP_K = 4
SWIGLU_LIMIT = 7.0
SWIGLU_ALPHA = 1.702
EPS = 1e-6
MAIN_WIDTH = 2 * M_HEADS * M_QK + 2 * M_WIDTH + 2 * G_HEADS * G_QK + 2 * G_WIDTH
SMALL_WIDTH = 128
NEG_INF = float("-inf")

VMEM_LIMIT = 56 * 1024 * 1024


def _cparams(sem):
    return pltpu.CompilerParams(dimension_semantics=sem, vmem_limit_bytes=VMEM_LIMIT)


def _log_sigmoid(x):
    return jnp.minimum(x, 0.0) - jnp.log1p(jnp.exp(-jnp.abs(x)))


def _split3(x):
    hi = x.astype(BF16)
    r = x - hi.astype(F32)
    mid = r.astype(BF16)
    lo = (r - mid.astype(F32)).astype(BF16)
    return hi, mid, lo


def _dot(a, b):
    return jnp.dot(a, b, preferred_element_type=F32)


def _dot_nt(a, b):
    return lax.dot_general(a, b, (((1,), (1,)), ((), ())), preferred_element_type=F32)


def _dot_tn(a, b):
    return lax.dot_general(a, b, (((0,), (0,)), ((), ())), preferred_element_type=F32)


U32 = jnp.uint32
HALF = D_MODEL // 2
TOP_BITS = 0xFFFF0000


def _pack_rows(x):
    xb = x.astype(BF16).astype(F32)
    hi = pltpu.bitcast(xb[:, :HALF], U32) & U32(TOP_BITS)
    lo = pltpu.bitcast(xb[:, HALF:], U32) >> 16
    return hi | lo


def _unpack_rows(p):
    return pltpu.bitcast(p & U32(TOP_BITS), F32), pltpu.bitcast(p << 16, F32)


def _split_hi_lo(w):
    hi = w.astype(BF16)
    return jnp.concatenate([hi, (w - hi.astype(F32)).astype(BF16)], axis=1)


def _cum_left(tri, x):
    hi, mid, lo = _split3(x)
    return _dot(tri, hi) + _dot(tri, mid) + _dot(tri, lo)


def _cum_right(x, tri):
    hi, mid, lo = _split3(x)
    return _dot(hi, tri) + _dot(mid, tri) + _dot(lo, tri)


def _inproj_kernel(xa_ref, xb_ref, nw_ref, w_ref, ws_ref, wst_ref, wkt_ref, main_ref, small_ref, smallt_ref, kt_ref,
                   hn_ref, *, nb_a):
    @pl.when(pl.program_id(1) == 0)
    def _():
        x = jnp.where(pl.program_id(0) < nb_a, xa_ref[...], xb_ref[...])
        y = x * lax.rsqrt(jnp.mean(x * x, axis=-1, keepdims=True) + EPS) * nw_ref[...]
        hb = y.astype(BF16)
        hn_ref[...] = hb
        small_ref[...] = _dot(hb, ws_ref[...])
        smallt_ref[...] = _dot_nt(wst_ref[...], hb)
        kt_ref[...] = _dot_nt(wkt_ref[...], hb).astype(BF16)

    main_ref[...] = _dot(hn_ref[...], w_ref[...]).astype(BF16)


def _row_specs(bm, nb_a, width):
    return [pl.BlockSpec((bm, width), lambda i, *_: (jnp.minimum(i, nb_a - 1), 0)),
            pl.BlockSpec((bm, width), lambda i, *_: (jnp.maximum(i - nb_a, 0), 0))]


def _in_proj(xa, xb, norm_w, w_main, w_small, w_small_t, w_kt, bm, bn):
    t = xa.shape[0] + xb.shape[0]
    nb_a = xa.shape[0] // bm
    return pl.pallas_call(
        functools.partial(_inproj_kernel, nb_a=nb_a),
        grid=(t // bm, MAIN_WIDTH // bn),
        in_specs=_row_specs(bm, nb_a, D_MODEL) + [
            pl.BlockSpec((1, D_MODEL), lambda i, j: (0, 0)),
            pl.BlockSpec((D_MODEL, bn), lambda i, j: (0, j)),
            pl.BlockSpec((D_MODEL, SMALL_WIDTH), lambda i, j: (0, 0)),
            pl.BlockSpec((SMALL_WIDTH, D_MODEL), lambda i, j: (0, 0)),
            pl.BlockSpec((M_HEADS * M_QK, D_MODEL), lambda i, j: (0, 0)),
        ],
        out_specs=[
            pl.BlockSpec((bm, bn), lambda i, j: (i, j)),
            pl.BlockSpec((bm, SMALL_WIDTH), lambda i, j: (i, 0)),
            pl.BlockSpec((SMALL_WIDTH, bm), lambda i, j: (0, i)),
            pl.BlockSpec((M_HEADS * M_QK, bm), lambda i, j: (0, i)),
        ],
        out_shape=[
            jax.ShapeDtypeStruct((t, MAIN_WIDTH), BF16),
            jax.ShapeDtypeStruct((t, SMALL_WIDTH), F32),
            jax.ShapeDtypeStruct((SMALL_WIDTH, t), F32),
            jax.ShapeDtypeStruct((M_HEADS * M_QK, t), BF16),
        ],
        scratch_shapes=[pltpu.VMEM((bm, D_MODEL), BF16)],
        compiler_params=_cparams(("arbitrary", "arbitrary")),
        name="in_proj",
    )(xa, xb, norm_w, w_main, w_small, w_small_t, w_kt)


def _chunk_masks(reverse):
    row = lax.broadcasted_iota(jnp.int32, (CHUNK, CHUNK), 0)
    col = lax.broadcasted_iota(jnp.int32, (CHUNK, CHUNK), 1)
    return (col >= row) if reverse else (col <= row)


def _reset_pred(blk, reset_blocks):
    pred = blk == reset_blocks[0]
    for rb in reset_blocks[1:]:
        pred = jnp.logical_or(pred, blk == rb)
    return pred


def _mlstm_kernel(q_ref, k_ref, kt_ref, v_ref, sm_ref, smt_ref, gb_ref, gbt_ref, tri_ref, trit_ref,
                  h_ref, c_ref, n_ref, m_ref, *, reverse, nblocks, reset_blocks, bt):
    i = pl.program_id(0)
    blk = (nblocks - 1 - i) if reverse else i

    @pl.when(_reset_pred(blk, reset_blocks))
    def _():
        c_ref[...] = jnp.zeros_like(c_ref)
        n_ref[...] = jnp.zeros_like(n_ref)
        m_ref[...] = jnp.zeros_like(m_ref)

    d = 1 if reverse else 0
    scale = M_QK ** -0.5
    gates = GATE_CAP * jnp.tanh((sm_ref[...] + gb_ref[...]) / GATE_CAP)
    bcum = _cum_left(tri_ref[...], _log_sigmoid(gates))
    gates_t = GATE_CAP * jnp.tanh((smt_ref[0:16, :] + gbt_ref[...]) / GATE_CAP)
    bcum_t = _cum_right(_log_sigmoid(gates_t), trit_ref[...])
    i_row = gates_t[8 * d:8 * d + 4, :]
    b_row = bcum_t[8 * d + 4:8 * d + 8, :]
    last = 0 if reverse else bt - 1
    g4 = b_row[:, last:last + 1]
    w4 = g4 - b_row + i_row
    mloc4 = jnp.max(w4, axis=1, keepdims=True)
    e4 = jnp.exp(w4 - mloc4)
    d4 = i_row - b_row
    row = lax.broadcasted_iota(jnp.int32, (bt, bt), 0)
    col = lax.broadcasted_iota(jnp.int32, (bt, bt), 1)
    mask = (col >= row) if reverse else (col <= row)
    ones_b = jnp.ones((bt, 128), BF16)
    e4b = e4.astype(BF16)
    outs = []
    for h in range(M_HEADS):
        fl = 8 * d + 4 + h
        g, m_loc = g4[h:h + 1], mloc4[h:h + 1]
        qb = q_ref[:, h * M_QK:(h + 1) * M_QK]
        kb = k_ref[:, h * M_QK:(h + 1) * M_QK]
        ktb = kt_ref[h * M_QK:(h + 1) * M_QK, :]
        vb = v_ref[:, h * M_V:(h + 1) * M_V]
        ekt = (ktb.astype(F32) * e4[h:h + 1]).astype(BF16)
        kv_loc = _dot(ekt, vb)
        n_loc = _dot(e4b, kb)[h:h + 1]
        s = _dot(qb, ktb) * scale
        b_b = jnp.broadcast_to(bcum[:, fl:fl + 1], (bt, 128))
        a_intra = jnp.where(mask, jnp.tile(b_b, (1, bt // 128)) + d4[h:h + 1], NEG_INF)
        m_intra = jnp.max(a_intra, axis=1, keepdims=True)
        pb = (jnp.exp(a_intra - m_intra) * s).astype(BF16)
        pv = _dot(pb, vb)
        p_sum = _dot(pb, ones_b)

        m_in, c_in, n_in = m_ref[h], c_ref[h], n_ref[h]
        m_intra_b = jnp.broadcast_to(m_intra, (bt, 128))
        a_inter = b_b + m_in
        m_row = jnp.maximum(m_intra_b, a_inter)
        f_intra = jnp.exp(m_intra_b - m_row)
        f_inter = jnp.exp(a_inter - m_row)
        qn = _dot_nt(qb, jnp.broadcast_to(n_in, (128, M_QK)).astype(BF16)) * scale
        den = f_intra * p_sum + f_inter * qn
        rinv = 1.0 / jnp.maximum(jnp.abs(den), jnp.exp(-m_row))
        qc = _dot(qb, c_in.astype(BF16)) * scale
        w_intra = jnp.tile(f_intra * rinv, (1, M_V // 128))
        w_inter = jnp.tile(f_inter * rinv, (1, M_V // 128))
        outs.append((w_intra * pv + w_inter * qc).astype(h_ref.dtype))

        m_new = jnp.maximum(g + m_in, m_loc)
        aa = jnp.exp(g + m_in - m_new)
        bb = jnp.exp(m_loc - m_new)
        c_ref[h] = aa * c_in + bb * kv_loc
        n_ref[h] = aa * n_in + bb * n_loc
        m_ref[h] = m_new
    h_ref[...] = jnp.concatenate(outs, axis=1)


def _gla_gate_kernel(sm_ref, wlr_ref, blr_ref, trif_ref, trib_ref, bf_ref, bb_ref):
    w = G_HEADS * G_QK
    sm = sm_ref[...]
    s_hi = sm.astype(BF16)
    s_lo = (sm - s_hi.astype(F32)).astype(BF16)
    r_hi = _dot(s_hi, wlr_ref[...])
    z = r_hi[:, :2 * w] + r_hi[:, 2 * w:] + _dot(s_lo, wlr_ref[:, :2 * w]) + blr_ref[...]
    log_a = _log_sigmoid(z) * (1.0 / G_TAU)
    bf_ref[...] = _cum_left(trif_ref[...], log_a[:, :w])
    bb_ref[...] = _cum_left(trib_ref[...], log_a[:, w:])


def _gla_gates(small, w_lr, b_lr, bt):
    t = small.shape[0]
    w = G_HEADS * G_QK
    wlr = jnp.zeros((SMALL_WIDTH, 2 * w), F32)
    wlr = wlr.at[16:32, :w].set(w_lr[0].astype(F32)).at[32:48, w:].set(w_lr[1].astype(F32))
    blr = jnp.concatenate([b_lr[0], b_lr[1]]).astype(F32).reshape(1, 2 * w)
    out = pl.BlockSpec((bt, w), lambda i: (i, 0))
    return pl.pallas_call(
        _gla_gate_kernel,
        grid=(t // bt,),
        in_specs=[pl.BlockSpec((bt, SMALL_WIDTH), lambda i: (i, 0)),
                  pl.BlockSpec((SMALL_WIDTH, 4 * w), lambda i: (0, 0)),
                  pl.BlockSpec((1, 2 * w), lambda i: (0, 0)),
                  pl.BlockSpec((bt, bt), lambda i: (0, 0)),
                  pl.BlockSpec((bt, bt), lambda i: (0, 0))],
        out_specs=[out, out],
        out_shape=[jax.ShapeDtypeStruct((t, w), F32), jax.ShapeDtypeStruct((t, w), F32)],
        compiler_params=_cparams(("arbitrary",)),
        name="gla_gates",
    )(small, _split_hi_lo(wlr), blr, _block_tri(bt, False), _block_tri(bt, True))


def _gla_kernel(q_ref, k_ref, v_ref, b_ref, o_ref, s_ref, *, reverse, nblocks, reset_blocks, bt):
    i = pl.program_id(0)
    blk = (nblocks - 1 - i) if reverse else i

    @pl.when(_reset_pred(blk, reset_blocks))
    def _():
        s_ref[...] = jnp.zeros_like(s_ref)

    scale = G_QK ** -0.5
    bcum = b_ref[...]
    mask = _chunk_masks(reverse)
    nchunks = bt // CHUNK
    order = range(nchunks - 1, -1, -1) if reverse else range(nchunks)
    sts = [s_ref[h] for h in range(G_HEADS)]
    for c in order:
        r0 = c * CHUNK
        outs = []
        for h in range(G_HEADS):
            b = bcum[r0:r0 + CHUNK, h * G_QK:(h + 1) * G_QK]
            g = b[0:1] if reverse else b[CHUNK - 1:CHUNK]
            qf = q_ref[r0:r0 + CHUNK, h * G_QK:(h + 1) * G_QK].astype(F32)
            kf = k_ref[r0:r0 + CHUNK, h * G_QK:(h + 1) * G_QK].astype(F32)
            vb = v_ref[r0:r0 + CHUNK, h * G_V:(h + 1) * G_V]
            q_dec = (qf * scale * jnp.exp(b)).astype(BF16)
            k_inv = (kf * jnp.exp(-b)).astype(BF16)
            k_end = (kf * jnp.exp(g - b)).astype(BF16)
            att = jnp.where(mask, _dot_nt(q_dec, k_inv), 0.0)
            o = _dot(att.astype(BF16), vb) + _dot_nt(q_dec, sts[h].astype(BF16))
            outs.append(o.astype(o_ref.dtype))
            sts[h] = sts[h] * jnp.exp(g) + _dot_tn(vb, k_end)
        o_ref[r0:r0 + CHUNK, :] = jnp.concatenate(outs, axis=1)
    for h in range(G_HEADS):
        s_ref[h] = sts[h]


def _block_tri(bt, upper, chunk=CHUNK):
    r = jnp.arange(bt)
    same = (r[:, None] // chunk) == (r[None, :] // chunk)
    tri = (r[None, :] >= r[:, None]) if upper else (r[None, :] <= r[:, None])
    return (same & tri).astype(BF16)


def _reset_blocks(seq_lens, bt, reverse):
    out, off = [], 0
    for s in seq_lens:
        assert s % bt == 0
        out.append((off + s) // bt - 1 if reverse else off // bt)
        off += s
    return tuple(out)


def _mlstm_dir(proj, k_t, small, small_t, gate_b, seq_lens, bt, reverse):
    t = proj.shape[0]
    nb = t // bt
    tri = _block_tri(bt, reverse, chunk=bt)
    rev = (lambda i: nb - 1 - i) if reverse else (lambda i: i)
    gb = jnp.zeros((1, SMALL_WIDTH), F32).at[0, :16].set(gate_b.reshape(16))
    gbt = gate_b.reshape(16, 1)
    kern = functools.partial(_mlstm_kernel, reverse=reverse, nblocks=nb,
                             reset_blocks=_reset_blocks(seq_lens, bt, reverse), bt=bt)
    return pl.pallas_call(
        kern,
        grid=(nb,),
        in_specs=[
            pl.BlockSpec((bt, 512), lambda i: (rev(i), 0)),
            pl.BlockSpec((bt, 512), lambda i: (rev(i), 1)),
            pl.BlockSpec((M_HEADS * M_QK, bt), lambda i: (0, rev(i))),
            pl.BlockSpec((bt, 1024), lambda i: (rev(i), 1)),
            pl.BlockSpec((bt, SMALL_WIDTH), lambda i: (rev(i), 0)),
            pl.BlockSpec((SMALL_WIDTH, bt), lambda i: (0, rev(i))),
            pl.BlockSpec((1, SMALL_WIDTH), lambda i: (0, 0)),
            pl.BlockSpec((16, 1), lambda i: (0, 0)),
            pl.BlockSpec((bt, bt), lambda i: (0, 0)),
            pl.BlockSpec((bt, bt), lambda i: (0, 0)),
        ],
        out_specs=pl.BlockSpec((bt, M_WIDTH), lambda i: (rev(i), 0)),
        out_shape=jax.ShapeDtypeStruct((t, M_WIDTH), BF16),
        scratch_shapes=[pltpu.VMEM((M_HEADS, M_QK, M_V), F32),
                        pltpu.VMEM((M_HEADS, 1, M_QK), F32),
                        pltpu.VMEM((M_HEADS, 1, 1), F32)],
        compiler_params=_cparams(("arbitrary",)),
        name="mlstm_bwd" if reverse else "mlstm_fwd",
    )(proj, proj, k_t, proj, small, small_t, gb, gbt, tri, tri.T)


def _gla_dir(proj, bcum, seq_lens, bt, reverse):
    t = proj.shape[0]
    nb = t // bt
    rev = (lambda i: nb - 1 - i) if reverse else (lambda i: i)
    kern = functools.partial(_gla_kernel, reverse=reverse, nblocks=nb,
                             reset_blocks=_reset_blocks(seq_lens, bt, reverse), bt=bt)
    return pl.pallas_call(
        kern,
        grid=(nb,),
        in_specs=[
            pl.BlockSpec((bt, 512), lambda i: (rev(i), 6)),
            pl.BlockSpec((bt, 512), lambda i: (rev(i), 7)),
            pl.BlockSpec((bt, 1024), lambda i: (rev(i), 4)),
            pl.BlockSpec((bt, G_HEADS * G_QK), lambda i: (rev(i), 0)),
        ],
        out_specs=pl.BlockSpec((bt, G_WIDTH), lambda i: (rev(i), 0)),
        out_shape=jax.ShapeDtypeStruct((t, G_WIDTH), BF16),
        scratch_shapes=[pltpu.VMEM((G_HEADS, G_V, G_QK), F32)],
        compiler_params=_cparams(("arbitrary",)),
        name="gla_bwd" if reverse else "gla_fwd",
    )(proj, proj, proj, bcum)


def _head_norm(hsum, w, nheads, hd):
    outs = []
    for h in range(nheads):
        x = hsum[:, h * hd:(h + 1) * hd]
        outs.append(x * lax.rsqrt(jnp.mean(x * x, axis=-1, keepdims=True) + EPS))
    return jnp.concatenate(outs, axis=-1) * w


def _outproj_kernel(xa_ref, xb_ref, mf_ref, mb_ref, gf_ref, gb_ref, mo_ref, gr_ref, mnw_ref, gnw_ref, wout_ref,
                    n2w_ref, rw_ref, rb_ref, lt_ref, x1_ref, xn_ref, idx_ref, gate_ref, rank_ref, cnt_ref,
                    *, n_experts, nb_a):
    hm = mf_ref[...].astype(F32) + mb_ref[...].astype(F32)
    a_out = _head_norm(hm, mnw_ref[...], M_HEADS, M_V) * jax.nn.sigmoid(mo_ref[...].astype(F32))
    hg = gf_ref[...].astype(F32) + gb_ref[...].astype(F32)
    r = gr_ref[...].astype(F32)
    b_out = _head_norm(hg, gnw_ref[...], G_HEADS, G_V) * (r * jax.nn.sigmoid(r))
    mixed = jnp.concatenate([a_out, b_out], axis=-1).astype(BF16)
    x = jnp.where(pl.program_id(0) < nb_a, xa_ref[...], xb_ref[...])
    x1 = x + _dot(mixed, wout_ref[...])
    x1_ref[...] = x1
    xn = x1 * lax.rsqrt(jnp.mean(x1 * x1, axis=-1, keepdims=True) + EPS) * n2w_ref[...]
    xn_ref[...] = _pack_rows(xn)
    x_hi = xn.astype(BF16)
    x_lo = (xn - x_hi.astype(F32)).astype(BF16)
    r_hi = _dot(x_hi, rw_ref[...])
    logits = (r_hi[:, :n_experts] + r_hi[:, n_experts:] + _dot(x_lo, rw_ref[:, :n_experts])
              + rb_ref[...])
    lane = lax.broadcasted_iota(jnp.int32, logits.shape, 1).astype(F32)
    vals, idxs = [], []
    cur = logits
    for _ in range(TOP_K):
        m = jnp.max(cur, axis=1, keepdims=True)
        sel = jnp.min(jnp.where(cur == m, lane, float(n_experts)), axis=1, keepdims=True)
        vals.append(m)
        idxs.append(sel)
        cur = jnp.where(lane == sel, NEG_INF, cur)
    exps = [jnp.exp(v - vals[0]) for v in vals]
    tot = exps[0] + exps[1] + exps[2] + exps[3]
    gate_ref[...] = jnp.concatenate(exps, axis=1) / tot
    idx_ref[...] = jnp.concatenate(idxs, axis=1).astype(jnp.int32)

    @pl.when(pl.program_id(0) == 0)
    def _():
        cnt_ref[...] = jnp.zeros_like(cnt_ref)

    hits = [lane == sel for sel in idxs]
    tok_oh = jnp.where(hits[0] | hits[1] | hits[2] | hits[3], 1.0, 0.0)
    before = _dot(lt_ref[...], tok_oh.astype(BF16)) + cnt_ref[...]
    rank_ref[...] = jnp.concatenate(
        [jnp.sum(jnp.where(hit, before, 0.0), axis=1, keepdims=True) for hit in hits], axis=1).astype(jnp.int32)
    cnt_ref[...] += jnp.sum(tok_oh, axis=0, keepdims=True)


def _out_proj(xa, xb, hm_f, hm_b, hg_f, hg_b, proj, mnw, gnw, w_out, n2w, router_w, router_b, bm):
    t = xa.shape[0] + xb.shape[0]
    nb_a = xa.shape[0] // bm
    n_e = router_w.shape[1] // 2
    row = lambda c: pl.BlockSpec((bm, 1024), lambda i: (i, c))
    full = lambda a: pl.BlockSpec(a.shape, lambda i: (0,) * a.ndim)
    r = jnp.arange(bm)
    lower = (r[None, :] < r[:, None]).astype(BF16)
    args = (xa, xb, hm_f, hm_b, hg_f, hg_b, proj, proj, mnw, gnw, w_out, n2w, router_w, router_b, lower)
    return pl.pallas_call(
        functools.partial(_outproj_kernel, n_experts=n_e, nb_a=nb_a),
        grid=(t // bm,),
        in_specs=_row_specs(bm, nb_a, D_MODEL) + [row(0), row(0), row(0), row(0), row(2), row(5)]
        + [full(a) for a in args[8:]],
        out_specs=[pl.BlockSpec((bm, D_MODEL), lambda i: (i, 0)),
                   pl.BlockSpec((bm, HALF), lambda i: (i, 0)),
                   pl.BlockSpec((bm, TOP_K), lambda i: (i, 0)),
                   pl.BlockSpec((bm, TOP_K), lambda i: (i, 0)),
                   pl.BlockSpec((bm, TOP_K), lambda i: (i, 0)),
                   pl.BlockSpec((1, n_e), lambda i: (0, 0))],
        out_shape=[jax.ShapeDtypeStruct((t, D_MODEL), F32),
                   jax.ShapeDtypeStruct((t, HALF), U32),
                   jax.ShapeDtypeStruct((t, TOP_K), jnp.int32),
                   jax.ShapeDtypeStruct((t, TOP_K), F32),
                   jax.ShapeDtypeStruct((t, TOP_K), jnp.int32),
                   jax.ShapeDtypeStruct((1, n_e), F32)],
        compiler_params=_cparams(("arbitrary",)),
        name="out_proj_router",
    )(*args)


DEINT = 512
WD_ROWS = 256


def _moe_kernel(be_ref, nu_ref, x_ref, wgu_ref, wd_ref, bgu_ref, bd_ref, out_ref, acc_ref, *, bf, nf):
    i = pl.program_id(0)
    f = pl.program_id(1)
    used = i < nu_ref[0]

    @pl.when(jnp.logical_and(jnp.logical_not(used), f == 0))
    def _():
        out_ref[...] = jnp.zeros_like(out_ref)

    @pl.when(used)
    def _():
        x_hi, x_lo = _unpack_rows(x_ref[...])
        xb = jnp.concatenate([x_hi.astype(BF16), x_lo.astype(BF16)], axis=1)
        g = jnp.minimum(_dot(xb, wgu_ref[0, 0, 0]) + bgu_ref[0, 0, :, :bf], SWIGLU_LIMIT)
        u = jnp.clip(_dot(xb, wgu_ref[0, 0, 1]) + bgu_ref[0, 0, :, bf:], -SWIGLU_LIMIT, SWIGLU_LIMIT)
        act = ((u + 1.0) * g * jax.nn.sigmoid(g * SWIGLU_ALPHA)).astype(BF16)
        contrib = _dot(act, wd_ref[0])

        @pl.when(f == 0)
        def _():
            acc_ref[...] = contrib + bd_ref[0]

        @pl.when(f != 0)
        def _():
            acc_ref[...] += contrib

        @pl.when(f == nf - 1)
        def _():
            out_ref[...] = _pack_rows(acc_ref[...])


def _moe_experts(x_sorted, block_e, n_used, wgu, wd, bgu, bd, bm):
    rows = x_sorted.shape[0]
    nblocks = rows // bm
    nf, bf = wgu.shape[1], wgu.shape[4]

    def fe(i, f, be, nu):
        return jnp.where(i < nu[0], f, nf - 1)

    grid_spec = pltpu.PrefetchScalarGridSpec(
        num_scalar_prefetch=2,
        grid=(nblocks, nf),
        in_specs=[
            pl.BlockSpec((bm, HALF), lambda i, f, be, nu: (jnp.minimum(i, nu[0] - 1), 0)),
            pl.BlockSpec((1, 1, 2, D_MODEL, bf), lambda i, f, be, nu: (be[i], fe(i, f, be, nu), 0, 0, 0)),
            pl.BlockSpec((1, bf, D_MODEL), lambda i, f, be, nu: (be[i], fe(i, f, be, nu), 0)),
            pl.BlockSpec((1, 1, 1, 2 * bf), lambda i, f, be, nu: (be[i], fe(i, f, be, nu), 0, 0)),
            pl.BlockSpec((1, 1, D_MODEL), lambda i, f, be, nu: (be[i], 0, 0)),
        ],
        out_specs=pl.BlockSpec((bm, HALF), lambda i, f, be, nu: (i, 0)),
        scratch_shapes=[pltpu.VMEM((bm, D_MODEL), F32)],
    )
    return pl.pallas_call(
        functools.partial(_moe_kernel, bf=bf, nf=nf),
        grid_spec=grid_spec,
        out_shape=jax.ShapeDtypeStruct((rows, HALF), U32),
        compiler_params=_cparams(("arbitrary", "arbitrary")),
        name="moe_experts",
    )(block_e, n_used, x_sorted, wgu, wd, bgu, bd)


def _row_copy(src_ref, src_row, dst_ref, dst_row, sem):
    return pltpu.make_async_copy(src_ref.at[pl.ds(src_row, 1), :], dst_ref.at[pl.ds(dst_row, 1), :], sem)


def _dispatch_kernel(pe_ref, dest_ref, xn_ref, wgu_ref, perm_ref, wd_ref, xs_ref, wgu_out_ref, wd_out_ref,
                     zero_ref, sem, zsem, *, bm, bm_moe, n_experts, n_tok_steps, n_w_steps):
    step = pl.program_id(0)

    @pl.when(step < n_w_steps)
    def _():
        res = _dot(wgu_ref[0].astype(BF16), perm_ref[...]).astype(BF16)
        wgu_out_ref[0, 0, 0] = res[:, :DEINT // 2]
        wgu_out_ref[0, 0, 1] = res[:, DEINT // 2:]
        wd_out_ref[...] = wd_ref[...].astype(BF16)

    @pl.when(step == 0)
    def _():
        zero_ref[...] = jnp.zeros_like(zero_ref)

        def last_block(e):
            start = pl.multiple_of(pe_ref[e + 1] - bm_moe, bm_moe)
            return pltpu.make_async_copy(zero_ref, xs_ref.at[pl.ds(start, bm_moe), :], zsem)

        for e in range(n_experts):
            @pl.when(pe_ref[e + 1] > pe_ref[e])
            def _():
                last_block(e).start()
        for e in range(n_experts):
            @pl.when(pe_ref[e + 1] > pe_ref[e])
            def _():
                last_block(e).wait()

        def unused_block(b, carry):
            cp = pltpu.make_async_copy(zero_ref, xs_ref.at[pl.ds(pl.multiple_of(b * bm_moe, bm_moe), bm_moe), :], zsem)
            cp.start()
            cp.wait()
            return carry
        lax.fori_loop(pe_ref[n_experts] // bm_moe, xs_ref.shape[0] // bm_moe, unused_block, 0)

    def issue(r, carry):
        for k in range(TOP_K):
            _row_copy(xn_ref, r, xs_ref, dest_ref[TOP_K * r + k], sem).start(priority=k % 2)
        return carry

    def drain(r, carry):
        for k in range(TOP_K):
            _row_copy(xn_ref, r, xs_ref, dest_ref[TOP_K * r + k], sem).wait()
        return carry

    @pl.when(step < n_tok_steps)
    def _():
        lax.fori_loop(0, bm, issue, 0, unroll=8)
        lax.fori_loop(0, bm, drain, 0, unroll=8)


def _dispatch_and_prep(xn, dest, pad_edges, rows, w_gate_up, w_down, bm, bm_moe, bf):
    t = xn.shape[0]
    n_e, d, two_f = w_gate_up.shape
    d_ff = two_f // 2
    half = DEINT // 2
    per_e = two_f // DEINT
    per_f = bf // half
    assert d_ff // WD_ROWS == per_e
    n_w = n_e * per_e
    n_t = t // bm
    r = jnp.arange(DEINT)[:, None]
    c = jnp.arange(DEINT)[None, :]
    perm = jnp.where(c < half, r == 2 * c, r == 2 * (c - half) + 1).astype(BF16)

    def tok(i):
        return jnp.minimum(i, n_t - 1)

    def wstep(i):
        return jnp.minimum(i, n_w - 1)

    grid_spec = pltpu.PrefetchScalarGridSpec(
        num_scalar_prefetch=1,
        grid=(max(n_t, n_w),),
        in_specs=[pl.BlockSpec((bm * TOP_K,), lambda i, pe: (tok(i),), memory_space=pltpu.SMEM),
                  pl.BlockSpec((bm, HALF), lambda i, pe: (tok(i), 0)),
                  pl.BlockSpec((1, d, DEINT), lambda i, pe: (wstep(i) // per_e, 0, wstep(i) % per_e)),
                  pl.BlockSpec((DEINT, DEINT), lambda i, pe: (0, 0)),
                  pl.BlockSpec((1, WD_ROWS, d), lambda i, pe: (wstep(i) // per_e, wstep(i) % per_e, 0))],
        out_specs=[pl.BlockSpec(memory_space=pl.ANY),
                   pl.BlockSpec((1, 1, 2, d, half),
                                lambda i, pe: (wstep(i) // per_e, (wstep(i) % per_e) // per_f, 0, 0, wstep(i) % per_f)),
                   pl.BlockSpec((1, WD_ROWS, d), lambda i, pe: (wstep(i) // per_e, wstep(i) % per_e, 0))],
        scratch_shapes=[pltpu.VMEM((bm_moe, HALF), U32),
                        pltpu.SemaphoreType.DMA(()), pltpu.SemaphoreType.DMA(())],
    )
    return pl.pallas_call(
        functools.partial(_dispatch_kernel, bm=bm, bm_moe=bm_moe, n_experts=n_e, n_tok_steps=n_t, n_w_steps=n_w),
        grid_spec=grid_spec,
        out_shape=[jax.ShapeDtypeStruct((rows, HALF), U32),
                   jax.ShapeDtypeStruct((n_e, d_ff // bf, 2, d, bf), BF16),
                   jax.ShapeDtypeStruct((n_e, d_ff, d), BF16)],
        compiler_params=_cparams(("arbitrary",)),
        name="dispatch_rows_prep_weights",
    )(pad_edges, dest, xn, w_gate_up, perm, w_down)


SUB = 8
ROWS = 64


def _combine_kernel(dcur_ref, dnxt_ref, x1_ref, gate_ref, fw_ref, ys_ref, outa_ref, outb_ref, ybuf0, ybuf1, sems,
                    *, bm, nb_a):
    i = pl.program_id(0)
    nb = pl.num_programs(0)
    bufs = (ybuf0, ybuf1)

    def row_copy(dref, s, grp, u, k):
        src_row = dref[TOP_K * (SUB * grp + u) + k]
        return pltpu.make_async_copy(ys_ref.at[pl.ds(src_row, 1), :], bufs[s].at[k, grp, pl.ds(u, 1), :], sems.at[s])

    def issue(dref, s, grp):
        for u in range(SUB):
            for k in range(TOP_K):
                row_copy(dref, s, grp, u, k).start(priority=k % 2)

    def drain(s):
        def body(grp, carry):
            for u in range(SUB):
                for k in range(TOP_K):
                    row_copy(dcur_ref, s, grp, u, k).wait()
            return carry
        lax.fori_loop(0, bm // SUB, body, 0)

    @pl.when(i == 0)
    def _():
        def body(grp, carry):
            issue(dcur_ref, 0, grp)
            return carry
        lax.fori_loop(0, bm // SUB, body, 0)

    def combine(s, out_ref):
        drain(s)

        def body(g, carry):
            r0 = pl.multiple_of(g * ROWS, ROWS)
            for j in range(ROWS // SUB):
                issue(dnxt_ref, 1 - s, g * (ROWS // SUB) + j)
            rows = pl.ds(r0, ROWS)
            gate = gate_ref[rows, :]
            tiles = pl.ds(g * (ROWS // SUB), ROWS // SUB)
            halves = [_unpack_rows(bufs[s][k, tiles].reshape(ROWS, HALF)) for k in range(TOP_K)]
            y = jnp.concatenate(
                [sum(halves[k][part] * gate[:, k:k + 1] for k in range(TOP_K)) for part in range(2)], axis=1)
            x2 = x1_ref[rows, :] + y
            out_ref[rows, :] = x2 * lax.rsqrt(jnp.mean(x2 * x2, axis=-1, keepdims=True) + EPS) * fw_ref[...]
            return carry
        lax.fori_loop(0, bm // ROWS, body, 0)

        @pl.when(i == nb - 1)
        def _():
            drain(1 - s)

    for s in range(2):
        for first, out_ref in ((True, outa_ref), (False, outb_ref)):
            @pl.when(jnp.logical_and(lax.rem(i, 2) == s, (i < nb_a) == first))
            def _():
                combine(s, out_ref)


def _combine(x1, ys, dest, gates, final_w, bm, rows_a):
    t = x1.shape[0]
    nb = t // bm
    nb_a = rows_a // bm
    blk = pl.BlockSpec((bm, D_MODEL), lambda i: (i, 0))
    return pl.pallas_call(
        functools.partial(_combine_kernel, bm=bm, nb_a=nb_a),
        grid=(nb,),
        in_specs=[pl.BlockSpec((bm * TOP_K,), lambda i: (i,), memory_space=pltpu.SMEM),
                  pl.BlockSpec((bm * TOP_K,), lambda i: (jnp.minimum(i + 1, nb - 1),), memory_space=pltpu.SMEM),
                  blk,
                  pl.BlockSpec((bm, TOP_K), lambda i: (i, 0)),
                  pl.BlockSpec((1, D_MODEL), lambda i: (0, 0)),
                  pl.BlockSpec(memory_space=pl.ANY)],
        out_specs=_row_specs(bm, nb_a, D_MODEL),
        out_shape=[jax.ShapeDtypeStruct((rows_a, D_MODEL), F32),
                   jax.ShapeDtypeStruct((t - rows_a, D_MODEL), F32)],
        scratch_shapes=[pltpu.VMEM((TOP_K, bm // SUB, SUB, HALF), U32)] * 2 + [
                        pltpu.SemaphoreType.DMA((2,))],
        compiler_params=_cparams(("arbitrary",)),
        name="combine_final_norm",
    )(dest, dest, x1, gates, final_w, ys)


def _route(top_idx, rank, counts, bm):
    n_experts = counts.shape[0]
    n_assign = top_idx.size
    padded = ((counts + bm - 1) // bm) * bm
    pad_end = jnp.cumsum(padded)
    pad_start = pad_end - padded
    dest = (jnp.take(pad_start, top_idx) + rank).astype(jnp.int32).reshape(n_assign)
    nblocks = n_assign // bm + n_experts
    n_used = (pad_end[-1] // bm).astype(jnp.int32).reshape(1)
    block_start = jnp.arange(nblocks, dtype=jnp.int32) * bm
    block_e = jnp.minimum(jnp.sum(pad_end[None, :] <= block_start[:, None], axis=1), n_experts - 1)
    last_e = block_e[jnp.maximum(n_used[0] - 1, 0)]
    block_e = jnp.where(jnp.arange(nblocks) < n_used[0], block_e, last_e).astype(jnp.int32)
    pad_edges = jnp.concatenate([jnp.zeros((1,), jnp.int32), pad_end.astype(jnp.int32)])
    return dest, pad_edges, block_e, n_used, nblocks * bm


def _trunk(xa, xb, seq_lens, norm1_w, w_in, mlstm_gate_b, mlstm_norm_w, gla_w_lr, gla_b_lr, gla_norm_w, w_out,
           norm2_w, router_w, router_b, w_gate_up, b_gate_up, w_down, b_down, final_norm_w,
           *, bm_in=512, bn_in=2048, bt_mix=256, bm_out=256, bm_disp=128, bm_moe=512, bf_moe=1024, bm_comb=256):
    n_e = router_w.shape[1]
    w_main = jnp.concatenate([w_in[:, :3072], w_in[:, 3088:6160]], axis=1).astype(BF16)
    w_small = jnp.zeros((D_MODEL, SMALL_WIDTH), F32)
    w_small = w_small.at[:, 0:16].set(w_in[:, 3072:3088]).at[:, 16:48].set(w_in[:, 6160:6192]).astype(BF16)
    w_kt = w_in[:, 512:1024].T.astype(BF16)
    proj, small, small_t, k_t = _in_proj(xa, xb, norm1_w.reshape(1, D_MODEL), w_main, w_small, w_small.T, w_kt,
                                         bm_in, bn_in)

    hm_f = _mlstm_dir(proj, k_t, small, small_t, mlstm_gate_b, seq_lens, bt_mix, False)
    hm_b = _mlstm_dir(proj, k_t, small, small_t, mlstm_gate_b, seq_lens, bt_mix, True)
    bcum_f, bcum_b = _gla_gates(small, gla_w_lr, gla_b_lr, bt_mix)
    hg_f = _gla_dir(proj, bcum_f, seq_lens, bt_mix, False)
    hg_b = _gla_dir(proj, bcum_b, seq_lens, bt_mix, True)

    x1, xn, top_idx, gates, rank, counts = _out_proj(
        xa, xb, hm_f, hm_b, hg_f, hg_b, proj, mlstm_norm_w.reshape(1, M_WIDTH), gla_norm_w.reshape(1, G_WIDTH),
        w_out.astype(BF16), norm2_w.reshape(1, D_MODEL), _split_hi_lo(router_w.astype(F32)),
        router_b.reshape(1, n_e).astype(F32), bm_out)

    dest, pad_edges, block_e, n_used, rows = _route(top_idx, rank, counts[0].astype(jnp.int32), bm_moe)
    x_sorted, wgu, wd = _dispatch_and_prep(xn, dest, pad_edges, rows, w_gate_up, w_down, bm_disp, bm_moe, bf_moe)
    nf = w_gate_up.shape[2] // (2 * bf_moe)
    bgu = jnp.concatenate([b_gate_up[:, 0::2].reshape(n_e, nf, 1, bf_moe),
                           b_gate_up[:, 1::2].reshape(n_e, nf, 1, bf_moe)], axis=-1).astype(F32)
    ys = _moe_experts(x_sorted, block_e, n_used, wgu, wd, bgu,
                      b_down[:, None, :].astype(F32), bm_moe)
    return _combine(x1, ys, dest, gates, final_norm_w.reshape(1, D_MODEL), bm_comb, xa.shape[0])


def kernel(x_prompt, x_sample, norm1_w, w_in, mlstm_gate_b, mlstm_norm_w, gla_w_lr, gla_b_lr, gla_norm_w,
           w_out, norm2_w, router_w, router_b, w_gate_up, b_gate_up, w_down, b_down, final_norm_w):
    bp, sp, _ = x_prompt.shape
    bs, ss, _ = x_sample.shape
    seq_lens = (sp,) * bp + (ss,) * bs
    ya, yb = _trunk(x_prompt.reshape(bp * sp, D_MODEL), x_sample.reshape(bs * ss, D_MODEL), seq_lens,
                    norm1_w[0], w_in[0], mlstm_gate_b[0], mlstm_norm_w[0], gla_w_lr[0], gla_b_lr[0],
                    gla_norm_w[0], w_out[0], norm2_w[0], router_w[0], router_b[0], w_gate_up[0], b_gate_up[0],
                    w_down[0], b_down[0], final_norm_w)
    return (ya.reshape(bp, sp, D_MODEL), yb.reshape(bs, ss, D_MODEL))
```
